```python
import jax, jax.numpy as jnp
from jax import lax
import numpy as np

D_MODEL = 2048
BATCH = 16
SEQ = 2048
DEPTH = 1
DEC_BATCH = 32
DEC_SEQ = 16
PAST_LEN = 1024

CHUNK = 64
Q_BLOCK = 128
NORM_EPS = 1e-6

RWKV_HEAD = 64
RWKV_HEADS = 16
RWKV_WIDTH = RWKV_HEADS * RWKV_HEAD
DECAY_LORA = 96
AAA_LORA = 96
GATE_LORA = 256
GN_EPS = 64e-5
RWKV_PROJ = 3 * RWKV_WIDTH + DECAY_LORA + AAA_LORA + GATE_LORA

MLA_HEADS = 8
QK_NOPE = 128
QK_ROPE = 64
V_HEAD = 128
KV_LORA = 512
ROPE_THETA = 10000.0
MLA_Q_WIDTH = MLA_HEADS * (QK_NOPE + QK_ROPE)
MLA_PROJ = MLA_Q_WIDTH + KV_LORA + QK_ROPE
MLA_WIDTH = MLA_HEADS * V_HEAD
SOFTMAX_SCALE = (QK_NOPE + QK_ROPE) ** -0.5
NEG_INF = -1e30

GATE_PROJ = 2 * D_MODEL
IN_PROJ = RWKV_PROJ + MLA_PROJ + GATE_PROJ

D_FF = 5632
CONV_W = 3

kernel_name = 'rwkv7_mla_convglu_streaming_step'


def rmsnorm(x, g):
    xf = x.astype(jnp.float32)
    y = xf * lax.rsqrt(jnp.mean(xf * xf, axis=-1, keepdims=True) + NORM_EPS)
    return (y * g.astype(jnp.float32)).astype(x.dtype)


def rope(x, pos):
    half = QK_ROPE // 2
    inv = ROPE_THETA ** (-jnp.arange(half, dtype=jnp.float32) / half)
    ang = pos.astype(jnp.float32)[:, None] * inv[None, :]
    shp = (pos.shape[0],) + (1,) * (x.ndim - 3) + (half,)
    cos, sin = jnp.cos(ang).reshape(shp), jnp.sin(ang).reshape(shp)
    xf = x.astype(jnp.float32)
    x1, x2 = xf[..., :half], xf[..., half:]
    return jnp.concatenate([x1 * cos - x2 * sin, x1 * sin + x2 * cos], axis=-1).astype(x.dtype)


def rwkv7_mix(p_a, prev_row, wkv0, mu_shift, w0, w2, a0, a2, g2, k_k, k_a, r_k, lnx_g, lnx_b, w_o_a):
    B, T, _ = p_a.shape
    H, N, W = RWKV_HEADS, RWKV_HEAD, RWKV_WIDTH
    f32 = jnp.float32
    p_prev = jnp.concatenate([prev_row.astype(p_a.dtype), p_a[:, :-1]], axis=1)
    z = p_a + mu_shift * (p_prev - p_a)
    r, k, v, zw, za, zg = jnp.split(z, [W, 2 * W, 3 * W, 3 * W + DECAY_LORA, 3 * W + DECAY_LORA + AAA_LORA], axis=-1)
    w_raw = (w0 + jnp.tanh(zw) @ w2).astype(f32)
    decay = jnp.exp(-jnp.exp(-jax.nn.softplus(-w_raw) - 0.5))
    a = jax.nn.sigmoid((a0 + za @ a2).astype(f32))
    g = jax.nn.sigmoid(zg) @ g2
    r, k, v = r.astype(f32), k.astype(f32), v.astype(f32)
    kk = (k * k_k).reshape(B, T, H, N)
    kk = kk * lax.rsqrt(jnp.maximum(jnp.sum(kk * kk, axis=-1, keepdims=True), 1e-12))
    k = k * (1.0 + (a - 1.0) * k_a)
    heads = lambda t: t.reshape(B, T, H, N)
    r, k, v, decay, a = heads(r), heads(k), heads(v), heads(decay), heads(a)

    def step(S, inp):
        r_t, w_t, k_t, v_t, kk_t, a_t = inp
        sa = jnp.einsum('bhvk,bhk->bhv', S, kk_t)
        S = (S * w_t[:, :, None, :] + v_t[..., None] * k_t[:, :, None, :]
             - sa[..., None] * (kk_t * a_t)[:, :, None, :])
        return S, jnp.einsum('bhvk,bhk->bhv', S, r_t)

    seq_first = lambda t: jnp.swapaxes(t, 0, 1)
    S_T, y = lax.scan(step, wkv0.astype(f32), (seq_first(r), seq_first(decay), seq_first(k),
                                                seq_first(v), seq_first(kk), seq_first(a)))
    y = seq_first(y)
    mu = jnp.mean(y, axis=-1, keepdims=True)
    var = jnp.mean(jnp.square(y - mu), axis=-1, keepdims=True)
    y = ((y - mu) * lax.rsqrt(var + GN_EPS)).reshape(B, T, W) * lnx_g + lnx_b
    bonus = jnp.sum(r * k * r_k, axis=-1, keepdims=True) * v
    y = (y + bonus.reshape(B, T, W)) * g
    return y.astype(p_a.dtype) @ w_o_a, S_T, p_a[:, -1:]


def mla_project(p_b, pos, g_kv):
    B, T, _ = p_b.shape
    q, c_kv, k_r = jnp.split(p_b, [MLA_Q_WIDTH, MLA_Q_WIDTH + KV_LORA], axis=-1)
    q = q.reshape(B, T, MLA_HEADS, QK_NOPE + QK_ROPE)
    return q[..., :QK_NOPE], rope(q[..., QK_NOPE:], pos), rmsnorm(c_kv, g_kv), rope(k_r, pos)


def mla_expand(c_kv, w_ukv):
    B, S, _ = c_kv.shape
    kv = (c_kv @ w_ukv).reshape(B, S, MLA_HEADS, QK_NOPE + V_HEAD)
    return kv[..., :QK_NOPE], kv[..., QK_NOPE:]


def attend(q_nope, q_rope, k_nope, k_rope, v, mask):
    s = jnp.einsum('bqhd,bkhd->bhqk', q_nope, k_nope) + jnp.einsum('bqhr,bkr->bhqk', q_rope, k_rope)
    s = s.astype(jnp.float32) * SOFTMAX_SCALE
    if mask is not None:
        s = jnp.where(mask[None, None], s, NEG_INF)
    p = jax.nn.softmax(s, axis=-1).astype(v.dtype)
    return jnp.einsum('bhqk,bkhd->bqhd', p, v)


def mla_prompt_attention(q_nope, q_rope, k_nope, k_rope, v):
    T = q_nope.shape[1]
    outs = []
    for start in range(0, T, Q_BLOCK):
        stop = start + Q_BLOCK
        q_chunk = (start + jnp.arange(Q_BLOCK)) // CHUNK
        k_chunk = jnp.arange(stop) // CHUNK
        mask = k_chunk[None, :] <= q_chunk[:, None]
        outs.append(attend(q_nope[:, start:stop], q_rope[:, start:stop],
                           k_nope[:, :stop], k_rope[:, :stop], v[:, :stop], mask))
    return jnp.concatenate(outs, axis=1)


def hybrid_layer(x, pos, shift_prev, wkv0, conv_prev, past_latent, past_krope, lp):
    B, T, _ = x.shape
    xn = rmsnorm(x, lp['g_mix'])
    proj = xn @ lp['w_in']
    p_a, p_b, p_g = jnp.split(proj, [RWKV_PROJ, RWKV_PROJ + MLA_PROJ], axis=-1)
    y_a, wkv_new, shift_new = rwkv7_mix(p_a, shift_prev, wkv0, lp['mu_shift'], lp['w0'], lp['w2'],
                                        lp['a0'], lp['a2'], lp['g2'], lp['k_k'], lp['k_a'], lp['r_k'],
                                        lp['lnx_g'], lp['lnx_b'], lp['w_o_a'])
    q_nope, q_rope, c_kv, k_rope = mla_project(p_b, pos, lp['g_kv'])
    if past_latent is None:
        k_nope, v = mla_expand(c_kv, lp['w_ukv'])
        o = mla_prompt_attention(q_nope, q_rope, k_nope, k_rope, v)
    else:
        lat = jnp.concatenate([past_latent.astype(c_kv.dtype), c_kv], axis=1)
        kr = jnp.concatenate([past_krope.astype(k_rope.dtype), k_rope], axis=1)
        k_nope, v = mla_expand(lat, lp['w_ukv'])
        o = attend(q_nope, q_rope, k_nope, kr, v, None)
    y_b = o.reshape(B, T, MLA_WIDTH) @ lp['w_o_b']
    gate_a, gate_b = jnp.split(jax.nn.sigmoid(p_g), 2, axis=-1)
    h = x + (gate_a * y_a + gate_b * y_b) @ lp['w_out']
    hn = rmsnorm(h, lp['g_ffn'])
    gate, up = jnp.split(hn @ lp['w_ffn_in'], 2, axis=-1)
    gate_ext = jnp.concatenate([conv_prev.astype(gate.dtype), gate], axis=1)
    cw = lp['conv_w']
    gate_c = sum(gate_ext[:, j:j + T] * cw[j] for j in range(CONV_W)) + lp['conv_b']
    h = h + (jax.nn.silu(gate_c) * up) @ lp['w_ffn_down']
    return h, c_kv, k_rope, wkv_new, shift_new, gate_ext[:, T:]


def setup_inputs(seed: int = 0) -> dict:
    key = jax.random.key(seed)
    ks = iter(jax.random.split(key, 40))
    nrm = lambda shape, scale: jax.random.normal(next(ks), shape, jnp.float32) * scale
    L, W, H, N = DEPTH, RWKV_WIDTH, RWKV_HEADS, RWKV_HEAD
    return {
        'x_prompt': nrm((BATCH, SEQ, D_MODEL), 1.0),
        'x_sample': nrm((DEC_BATCH, DEC_SEQ, D_MODEL), 1.0),
        'cache_mla_latent': nrm((L, DEC_BATCH, PAST_LEN, KV_LORA), 1.0),
        'cache_mla_krope': nrm((L, DEC_BATCH, PAST_LEN, QK_ROPE), 1.0),
        'state_rwkv_wkv': nrm((L, DEC_BATCH, H, N, N), 0.3),
        'state_rwkv_shift': nrm((L, DEC_BATCH, 1, RWKV_PROJ), 1.0),
        'state_ffn_conv': nrm((L, DEC_BATCH, CONV_W - 1, D_FF), 1.0),
        'g_mix': 1.0 + nrm((L, D_MODEL), 0.02),
        'w_in': nrm((L, D_MODEL, IN_PROJ), D_MODEL ** -0.5),
        'mu_shift': jax.random.uniform(next(ks), (L, RWKV_PROJ), jnp.float32),
        'w0': jax.random.uniform(next(ks), (L, W), jnp.float32, -5.0, 1.0),
        'w2': nrm((L, DECAY_LORA, W), DECAY_LORA ** -0.5),
        'a0': nrm((L, W), 0.3),
        'a2': nrm((L, AAA_LORA, W), AAA_LORA ** -0.5),
        'g2': nrm((L, GATE_LORA, W), GATE_LORA ** -0.5),
        'k_k': 0.85 + nrm((L, W), 0.05),
        'k_a': 1.0 + nrm((L, W), 0.05),
        'r_k': nrm((L, H, N), 0.1),
        'lnx_g': 1.0 + nrm((L, W), 0.02),
        'lnx_b': nrm((L, W), 0.02),
        'w_o_a': nrm((L, W, D_MODEL), W ** -0.5),
        'g_kv': 1.0 + nrm((L, KV_LORA), 0.02),
        'w_ukv': nrm((L, KV_LORA, MLA_HEADS * (QK_NOPE + V_HEAD)), KV_LORA ** -0.5),
        'w_o_b': nrm((L, MLA_WIDTH, D_MODEL), MLA_WIDTH ** -0.5),
        'w_out': nrm((L, D_MODEL, D_MODEL), D_MODEL ** -0.5),
        'g_ffn': 1.0 + nrm((L, D_MODEL), 0.02),
        'w_ffn_in': nrm((L, D_MODEL, 2 * D_FF), D_MODEL ** -0.5),
        'conv_w': nrm((L, CONV_W, D_FF), CONV_W ** -0.5),
        'conv_b': nrm((L, D_FF), 0.02),
        'w_ffn_down': nrm((L, D_FF, D_MODEL), D_FF ** -0.5),
        'g_final': 1.0 + nrm((D_MODEL,), 0.02),
    }


def reference(x_prompt, x_sample, cache_mla_latent, cache_mla_krope, state_rwkv_wkv, state_rwkv_shift,
              state_ffn_conv, g_mix, w_in, mu_shift, w0, w2, a0, a2, g2, k_k, k_a, r_k, lnx_g, lnx_b,
              w_o_a, g_kv, w_ukv, w_o_b, w_out, g_ffn, w_ffn_in, conv_w, conv_b, w_ffn_down, g_final):
    B, T, _ = x_prompt.shape
    Ts = x_sample.shape[1]
    past = cache_mla_latent.shape[2]
    pos_p = jnp.arange(T)
    pos_s = past + jnp.arange(Ts)
    dt = x_prompt.dtype
    h_p, h_s = x_prompt, x_sample
    st_p, st_s = [], []
    for l in range(DEPTH):
        lp = dict(g_mix=g_mix[l], w_in=w_in[l], mu_shift=mu_shift[l], w0=w0[l], w2=w2[l], a0=a0[l],
                  a2=a2[l], g2=g2[l], k_k=k_k[l], k_a=k_a[l], r_k=r_k[l], lnx_g=lnx_g[l], lnx_b=lnx_b[l],
                  w_o_a=w_o_a[l], g_kv=g_kv[l], w_ukv=w_ukv[l], w_o_b=w_o_b[l], w_out=w_out[l],
                  g_ffn=g_ffn[l], w_ffn_in=w_ffn_in[l], conv_w=conv_w[l], conv_b=conv_b[l],
                  w_ffn_down=w_ffn_down[l])
        zero_shift = jnp.zeros((B, 1, RWKV_PROJ), dt)
        zero_wkv = jnp.zeros((B, RWKV_HEADS, RWKV_HEAD, RWKV_HEAD), jnp.float32)
        zero_conv = jnp.zeros((B, CONV_W - 1, D_FF), dt)
        h_p, *sp = hybrid_layer(h_p, pos_p, zero_shift, zero_wkv, zero_conv, None, None, lp)
        h_s, *ss = hybrid_layer(h_s, pos_s, state_rwkv_shift[l], state_rwkv_wkv[l], state_ffn_conv[l],
                                cache_mla_latent[l], cache_mla_krope[l], lp)
        st_p.append(sp)
        st_s.append(ss)
    y_prompt = rmsnorm(h_p, g_final)
    y_sample = rmsnorm(h_s, g_final)
    stk = lambda outs, i: jnp.stack([o[i] for o in outs]).astype(dt)
    return (y_prompt, y_sample,
            stk(st_p, 0), stk(st_p, 1), stk(st_p, 2), stk(st_p, 3), stk(st_p, 4),
            stk(st_s, 0), stk(st_s, 1), stk(st_s, 2), stk(st_s, 3), stk(st_s, 4))
```

```python
import functools
import math

import jax
import jax.numpy as jnp
from jax import lax
from jax.experimental import pallas as pl
from jax.experimental.pallas import tpu as pltpu

F32 = jnp.float32
BF16 = jnp.bfloat16

D_MODEL = 2048
NORM_EPS = 1e-6
HEAD = 64
N_HEADS = 16
RW = N_HEADS * HEAD
DECAY_LORA = 96
AAA_LORA = 96
GATE_LORA = 256
GN_EPS = 64e-5
RWKV_PROJ = 3 * RW + DECAY_LORA + AAA_LORA + GATE_LORA
MLA_HEADS = 8
QK_NOPE = 128
QK_ROPE = 64
V_HEAD = 128
KV_LORA = 512
ROPE_THETA = 10000.0
MLA_Q_WIDTH = MLA_HEADS * (QK_NOPE + QK_ROPE)
MLA_PROJ = MLA_Q_WIDTH + KV_LORA + QK_ROPE
SOFTMAX_SCALE = (QK_NOPE + QK_ROPE) ** -0.5
NEG_INF = -1e30
CHUNK = 64
D_FF = 5632
CONV_W = 3

LANE = 128
LORA_PAD = 128
PA_W = 3 * RW + 2 * LORA_PAD + GATE_LORA
QR_W = MLA_HEADS * QK_ROPE
PB_W = MLA_HEADS * QK_NOPE + QR_W + KV_LORA + LANE
QK_CAT = 2 * LANE
VMEM_LIMIT = 56 * 1024 * 1024


def _cparams(sem, vmem=VMEM_LIMIT):
    return pltpu.CompilerParams(dimension_semantics=sem, vmem_limit_bytes=vmem)


def _dot(a, b):
    return jnp.dot(a.astype(BF16), b.astype(BF16), preferred_element_type=F32)


def _dot_nt(a, b):
    return lax.dot_general(a.astype(BF16), b.astype(BF16), (((1,), (1,)), ((), ())),
                           preferred_element_type=F32)


def _dot_tn(a, b):
    return lax.dot_general(a.astype(BF16), b.astype(BF16), (((0,), (0,)), ((), ())),
                           preferred_element_type=F32)


def _rms(x, g):
    return x * lax.rsqrt(jnp.mean(x * x, axis=-1, keepdims=True) + NORM_EPS) * g


def _norm_matmul_body(x_ref, g_ref, w_ref, o_ref, xn_ref):
    @pl.when(pl.program_id(1) == 0)
    def _():
        xn_ref[...] = _rms(x_ref[...], g_ref[...]).astype(BF16)

    o_ref[...] = jnp.dot(xn_ref[...], w_ref[...], preferred_element_type=F32).astype(o_ref.dtype)


def norm_matmul(x, g, w, tm, tn, out_dtype=F32):
    n, k = x.shape
    m = w.shape[1]
    return pl.pallas_call(
        _norm_matmul_body,
        grid=(n // tm, m // tn),
        in_specs=[pl.BlockSpec((tm, k), lambda i, j: (i, 0)),
                  pl.BlockSpec((1, k), lambda i, j: (0, 0)),
                  pl.BlockSpec((k, tn), lambda i, j: (0, j))],
        out_specs=pl.BlockSpec((tm, tn), lambda i, j: (i, j)),
        out_shape=jax.ShapeDtypeStruct((n, m), out_dtype),
        scratch_shapes=[pltpu.VMEM((tm, k), BF16)],
        compiler_params=_cparams(("parallel", "arbitrary")),
        name="norm_matmul",
    )(x, g.reshape(1, k), w)


def _rwkv_body(pa_ref, shift_ref, wkv0_ref, mu_ref, w0_ref, w2_ref, a0_ref, a2_ref, g2_ref,
               kk_ref, ka_ref, rk_ref, lng_ref, lnb_ref, y_ref, wkv_ref, st_ref, prev_ref, *, c):
    ci = pl.program_id(1)
    nc = pl.num_programs(1)

    @pl.when(ci == 0)
    def _():
        prev_ref[...] = shift_ref[0]
        for h in range(N_HEADS):
            st_ref[h] = wkv0_ref[0, h].T

    pa = pa_ref[...]
    row = lax.broadcasted_iota(jnp.int32, (c, 1), 0)
    p_prev = jnp.where(row == 0, prev_ref[0:1, :], pltpu.roll(pa, 1, 0))
    prev_ref[0:1, :] = pa[c - 1:c, :]
    z = pa + mu_ref[...] * (p_prev - pa)

    r_all = z[:, 0:RW]
    k_all = z[:, RW:2 * RW]
    v_all = z[:, 2 * RW:3 * RW]
    zw = z[:, 3 * RW:3 * RW + LORA_PAD]
    za = z[:, 3 * RW + LORA_PAD:3 * RW + 2 * LORA_PAD]
    zg = z[:, 3 * RW + 2 * LORA_PAD:PA_W]

    w_raw = w0_ref[...] + _dot(jnp.tanh(zw), w2_ref[...])
    logw = -jnp.exp(-jax.nn.softplus(-w_raw) - 0.5)
    a_all = jax.nn.sigmoid(a0_ref[...] + _dot(za, a2_ref[...]))
    g_all = _dot(jax.nn.sigmoid(zg), g2_ref[...])
    kk_all = k_all * kk_ref[...]
    k2_all = k_all * (1.0 + (a_all - 1.0) * ka_ref[...])

    ri = lax.broadcasted_iota(jnp.int32, (c, c), 0)
    cj = lax.broadcasted_iota(jnp.int32, (c, c), 1)
    incl = ri >= cj
    strict = ri > cj
    cum_all = jnp.dot(incl.astype(F32), logw, precision=lax.Precision.HIGHEST,
                      preferred_element_type=F32)
    eye_h = (lax.broadcasted_iota(jnp.int32, (HEAD, HEAD), 0)
             == lax.broadcasted_iota(jnp.int32, (HEAD, HEAD), 1))

    outs = []
    for h in range(N_HEADS):
        hs = slice(h * HEAD, (h + 1) * HEAD)
        lw = logw[:, hs]
        cum = cum_all[:, hs]
        gam = jnp.exp(cum)
        gam_prev = jnp.exp(cum - lw)
        inv_gam = jnp.exp(-cum)
        gam_c = jnp.exp(cum[c - 1:c, :])
        r = r_all[:, hs]
        k2 = k2_all[:, hs]
        v = v_all[:, hs]
        kk = kk_all[:, hs]
        kk = kk * lax.rsqrt(jnp.maximum(jnp.sum(kk * kk, axis=-1, keepdims=True), 1e-12))
        b = kk * a_all[:, hs]

        at = -kk * gam_prev
        bt = b * inv_gam
        kt = k2 * inv_gam
        rt = r * gam
        p = _dot_nt(jnp.concatenate([at, rt], axis=0), jnp.concatenate([bt, kt], axis=0))
        a_ab = jnp.where(strict, p[0:c, 0:c], 0.0)
        a_ak = jnp.where(strict, p[0:c, c:2 * c], 0.0)
        a_br = jnp.where(incl, p[c:2 * c, 0:c], 0.0)
        a_kr = jnp.where(incl, p[c:2 * c, c:2 * c], 0.0)

        x = jnp.where(ri == cj, 1.0, a_ab)
        pw = a_ab
        for _ in range(int(math.log2(c)) - 1):
            pw = _dot(pw, pw)
            x = x + _dot(x, pw)

        sol = _dot(x, jnp.concatenate([at, _dot(a_ak, v)], axis=1))
        asol = _dot(a_br, sol)
        rh = rt + asol[:, 0:HEAD]
        yh = asol[:, HEAD:2 * HEAD] + _dot(a_kr, v)
        mn = _dot_tn(bt * gam_c, sol)
        m_mat = jnp.where(eye_h, gam_c, 0.0) + mn[:, 0:HEAD]
        n_mat = mn[:, HEAD:2 * HEAD] + _dot_tn(kt * gam_c, v)

        st = st_ref[h]
        y = _dot(rh, st) + yh
        st_ref[h] = _dot(m_mat, st) + n_mat

        mu = jnp.mean(y, axis=-1, keepdims=True)
        yc = y - mu
        var = jnp.mean(yc * yc, axis=-1, keepdims=True)
        yn = yc * lax.rsqrt(var + GN_EPS) * lng_ref[:, hs] + lnb_ref[:, hs]
        bonus = jnp.sum(r * k2 * rk_ref[:, hs], axis=-1, keepdims=True) * v
        outs.append((yn + bonus) * g_all[:, hs])

    y_ref[...] = jnp.concatenate(outs, axis=1).astype(y_ref.dtype)

    @pl.when(ci == nc - 1)
    def _():
        for h in range(N_HEADS):
            wkv_ref[0, h] = st_ref[h].T


def rwkv_mix(pa, shift_prev, wkv0, prm, bsz, t, c):
    nc = t // c
    vec = lambda w: pl.BlockSpec((1, w), lambda b, i: (0, 0))
    mat = lambda r, w: pl.BlockSpec((r, w), lambda b, i: (0, 0))
    return pl.pallas_call(
        functools.partial(_rwkv_body, c=c),
        grid=(bsz, nc),
        in_specs=[pl.BlockSpec((c, PA_W), lambda b, i: (b * nc + i, 0)),
                  pl.BlockSpec((1, 8, PA_W), lambda b, i: (b, 0, 0)),
                  pl.BlockSpec((1, N_HEADS, HEAD, HEAD), lambda b, i: (b, 0, 0, 0)),
                  vec(PA_W), vec(RW), mat(LORA_PAD, RW), vec(RW), mat(LORA_PAD, RW), mat(GATE_LORA, RW),
                  vec(RW), vec(RW), vec(RW), vec(RW), vec(RW)],
        out_specs=[pl.BlockSpec((c, RW), lambda b, i: (b * nc + i, 0)),
                   pl.BlockSpec((1, N_HEADS, HEAD, HEAD), lambda b, i: (b, 0, 0, 0))],
        out_shape=[jax.ShapeDtypeStruct((bsz * t, RW), BF16),
                   jax.ShapeDtypeStruct((bsz, N_HEADS, HEAD, HEAD), F32)],
        scratch_shapes=[pltpu.VMEM((N_HEADS, HEAD, HEAD), F32), pltpu.VMEM((8, PA_W), F32)],
        compiler_params=_cparams(("parallel", "arbitrary")),
        name="rwkv_mix",
    )(pa, shift_prev, wkv0, prm["mu_a"], prm["w0"], prm["w2"], prm["a0"], prm["a2"], prm["g2"],
      prm["k_k"], prm["k_a"], prm["r_k"], prm["lnx_g"], prm["lnx_b"])


def _rope_swap(slab):
    lane = lax.broadcasted_iota(jnp.int32, slab.shape, 1)
    return jnp.where(lane % QK_ROPE < QK_ROPE // 2, pltpu.roll(slab, LANE - QK_ROPE // 2, 1),
                     pltpu.roll(slab, QK_ROPE // 2, 1))


def _expand_store(lat_bf, kslab_bf, wukv_ref, kcat_ref, v_ref):
    kv = jnp.dot(lat_bf, wukv_ref[...], preferred_element_type=F32)
    for h in range(MLA_HEADS):
        kcat_ref[:, h * QK_CAT:h * QK_CAT + LANE] = kv[:, h * QK_NOPE:(h + 1) * QK_NOPE].astype(BF16)
        kcat_ref[:, h * QK_CAT + LANE:(h + 1) * QK_CAT] = kslab_bf
    v_ref[...] = kv[:, MLA_HEADS * QK_NOPE:].astype(BF16)


def _mla_prep_body(x_ref, g_ref, wb_ref, cs_ref, sn_ref, gkv_ref, wukv_ref,
                   qcat_ref, lat_ref, kr_ref, kcat_ref, v_ref):
    xn = _rms(x_ref[...], g_ref[...]).astype(BF16)
    pb = jnp.dot(xn, wb_ref[...], preferred_element_type=F32)
    cs = cs_ref[...]
    sn = sn_ref[...]
    lane = lax.broadcasted_iota(jnp.int32, cs.shape, 1)
    low = lane < QK_ROPE
    nq = MLA_HEADS * QK_NOPE
    for p in range(MLA_HEADS // 2):
        slab = pb[:, nq + p * LANE:nq + (p + 1) * LANE]
        roped = slab * cs + _rope_swap(slab) * sn
        for j in range(2):
            h = 2 * p + j
            qcat_ref[:, h * QK_CAT:h * QK_CAT + LANE] = pb[:, h * QK_NOPE:(h + 1) * QK_NOPE].astype(BF16)
            half = roped if j == 0 else pltpu.roll(roped, QK_ROPE, 1)
            qcat_ref[:, h * QK_CAT + LANE:(h + 1) * QK_CAT] = jnp.where(low, half, 0.0).astype(BF16)
    lat = _rms(pb[:, nq + QR_W:nq + QR_W + KV_LORA], gkv_ref[...])
    lat_ref[...] = lat
    kslab = pb[:, nq + QR_W + KV_LORA:PB_W]
    kroped = jnp.where(low, kslab * cs + _rope_swap(kslab) * sn, 0.0)
    kr_ref[...] = kroped[:, 0:QK_ROPE]
    _expand_store(lat.astype(BF16), kroped.astype(BF16), wukv_ref, kcat_ref, v_ref)


def mla_prep(x, g_mix, w_b, cs, sn, g_kv, w_ukv, tm):
    n = x.shape[0]
    ntab = cs.shape[0] // tm
    full = lambda a: pl.BlockSpec(a.shape, lambda i: (0, 0))
    return pl.pallas_call(
        _mla_prep_body,
        grid=(n // tm,),
        in_specs=[pl.BlockSpec((tm, D_MODEL), lambda i: (i, 0)), full(g_mix), full(w_b),
                  pl.BlockSpec((tm, LANE), lambda i: (i % ntab, 0)),
                  pl.BlockSpec((tm, LANE), lambda i: (i % ntab, 0)), full(g_kv), full(w_ukv)],
        out_specs=[pl.BlockSpec((tm, MLA_HEADS * QK_CAT), lambda i: (i, 0)),
                   pl.BlockSpec((tm, KV_LORA), lambda i: (i, 0)),
                   pl.BlockSpec((tm, QK_ROPE), lambda i: (i, 0)),
                   pl.BlockSpec((tm, MLA_HEADS * QK_CAT), lambda i: (i, 0)),
                   pl.BlockSpec((tm, MLA_HEADS * V_HEAD), lambda i: (i, 0))],
        out_shape=[jax.ShapeDtypeStruct((n, MLA_HEADS * QK_CAT), BF16),
                   jax.ShapeDtypeStruct((n, KV_LORA), F32),
                   jax.ShapeDtypeStruct((n, QK_ROPE), F32),
                   jax.ShapeDtypeStruct((n, MLA_HEADS * QK_CAT), BF16),
                   jax.ShapeDtypeStruct((n, MLA_HEADS * V_HEAD), BF16)],
        compiler_params=_cparams(("parallel",)),
        name="mla_prep",
    )(x, g_mix, w_b, cs, sn, g_kv, w_ukv)


def _mla_expand_body(lat_ref, kr_ref, pad_ref, wukv_ref, kcat_ref, v_ref):
    kslab = jnp.dot(kr_ref[...].astype(BF16), pad_ref[...], preferred_element_type=F32).astype(BF16)
    _expand_store(lat_ref[...].astype(BF16), kslab, wukv_ref, kcat_ref, v_ref)


def mla_expand(lat, kr, w_ukv, tm):
    n = lat.shape[0]
    pad = jnp.eye(QK_ROPE, LANE, dtype=BF16)
    return pl.pallas_call(
        _mla_expand_body,
        grid=(n // tm,),
        in_specs=[pl.BlockSpec((tm, KV_LORA), lambda i: (i, 0)),
                  pl.BlockSpec((tm, QK_ROPE), lambda i: (i, 0)),
                  pl.BlockSpec(pad.shape, lambda i: (0, 0)),
                  pl.BlockSpec(w_ukv.shape, lambda i: (0, 0))],
        out_specs=[pl.BlockSpec((tm, MLA_HEADS * QK_CAT), lambda i: (i, 0)),
                   pl.BlockSpec((tm, MLA_HEADS * V_HEAD), lambda i: (i, 0))],
        out_shape=[jax.ShapeDtypeStruct((n, MLA_HEADS * QK_CAT), BF16),
                   jax.ShapeDtypeStruct((n, MLA_HEADS * V_HEAD), BF16)],
        compiler_params=_cparams(("parallel",)),
        name="mla_expand",
    )(lat, kr, pad, w_ukv)


def _softmax_block(s, v, m, l, acc):
    m_new = jnp.maximum(m, jnp.max(s, axis=-1, keepdims=True))
    alpha = jnp.exp(m - m_new)
    p = jnp.exp(s - m_new)
    l_new = alpha * l + jnp.sum(p, axis=-1, keepdims=True)
    acc_new = alpha * acc + jnp.dot(p.astype(BF16), v, preferred_element_type=F32)
    return m_new, l_new, acc_new


def _attn_prompt_body(q_ref, k_ref, v_ref, o_ref, *, tq):
    i = pl.program_id(2)
    q = q_ref[...]

    def scores(kblk):
        return lax.dot_general(q, kblk, (((1,), (1,)), ((), ())), preferred_element_type=F32) * SOFTMAX_SCALE

    start = pl.multiple_of(i * tq, tq)
    s = scores(k_ref[pl.ds(start, tq), :])
    qc = lax.broadcasted_iota(jnp.int32, (tq, tq), 0) // CHUNK
    kc = lax.broadcasted_iota(jnp.int32, (tq, tq), 1) // CHUNK
    s = jnp.where(kc <= qc, s, NEG_INF)
    m = jnp.max(s, axis=-1, keepdims=True)
    p = jnp.exp(s - m)
    l = jnp.sum(p, axis=-1, keepdims=True)
    acc = jnp.dot(p.astype(BF16), v_ref[pl.ds(start, tq), :], preferred_element_type=F32)

    def body(j, carry):
        off = pl.multiple_of(j * tq, tq)
        return _softmax_block(scores(k_ref[pl.ds(off, tq), :]), v_ref[pl.ds(off, tq), :], *carry)

    m, l, acc = lax.fori_loop(0, i, body, (m, l, acc))
    o_ref[...] = (acc / l).astype(o_ref.dtype)


def attention_prompt(qcat, kcat, v, bsz, t, tq):
    nq = t // tq
    return pl.pallas_call(
        functools.partial(_attn_prompt_body, tq=tq),
        grid=(bsz, MLA_HEADS, nq),
        in_specs=[pl.BlockSpec((tq, QK_CAT), lambda b, h, i: (b * nq + i, h)),
                  pl.BlockSpec((t, QK_CAT), lambda b, h, i: (b, h)),
                  pl.BlockSpec((t, V_HEAD), lambda b, h, i: (b, h))],
        out_specs=pl.BlockSpec((tq, V_HEAD), lambda b, h, i: (b * nq + i, h)),
        out_shape=jax.ShapeDtypeStruct((bsz * t, MLA_HEADS * V_HEAD), BF16),
        compiler_params=_cparams(("parallel", "parallel", "arbitrary")),
        name="attention_prompt",
    )(qcat, kcat, v)


def _attn_sample_body(q_ref, kp_ref, vp_ref, kn_ref, vn_ref, o_ref):
    q = q_ref[...]
    sc = lambda kblk: lax.dot_general(q, kblk, (((1,), (1,)), ((), ())),
                                      preferred_element_type=F32) * SOFTMAX_SCALE
    s = sc(kp_ref[...])
    m = jnp.max(s, axis=-1, keepdims=True)
    p = jnp.exp(s - m)
    l = jnp.sum(p, axis=-1, keepdims=True)
    acc = jnp.dot(p.astype(BF16), vp_ref[...], preferred_element_type=F32)
    m, l, acc = _softmax_block(sc(kn_ref[...]), vn_ref[...], m, l, acc)
    o_ref[...] = (acc / l).astype(o_ref.dtype)


def attention_sample(qcat, kcat_past, v_past, kcat_new, v_new, bsz, t, past):
    return pl.pallas_call(
        _attn_sample_body,
        grid=(bsz, MLA_HEADS),
        in_specs=[pl.BlockSpec((t, QK_CAT), lambda b, h: (b, h)),
                  pl.BlockSpec((past, QK_CAT), lambda b, h: (b, h)),
                  pl.BlockSpec((past, V_HEAD), lambda b, h: (b, h)),
                  pl.BlockSpec((t, QK_CAT), lambda b, h: (b, h)),
                  pl.BlockSpec((t, V_HEAD), lambda b, h: (b, h))],
        out_specs=pl.BlockSpec((t, V_HEAD), lambda b, h: (b, h)),
        out_shape=jax.ShapeDtypeStruct((bsz * t, MLA_HEADS * V_HEAD), BF16),
        compiler_params=_cparams(("parallel", "parallel")),
        name="attention_sample",
    )(qcat, kcat_past, v_past, kcat_new, v_new)


def _combine_body(ya_ref, ob_ref, pg_ref, x_ref, woa_ref, wob_ref, wout_ref, gffn_ref, h_ref, hn_ref):
    y_a = jnp.dot(ya_ref[...], woa_ref[...], preferred_element_type=F32)
    y_b = jnp.dot(ob_ref[...], wob_ref[...], preferred_element_type=F32)
    gate_a = jax.nn.sigmoid(pg_ref[:, 0:D_MODEL])
    gate_b = jax.nn.sigmoid(pg_ref[:, D_MODEL:2 * D_MODEL])
    mix = (gate_a * y_a + gate_b * y_b).astype(BF16)
    h = x_ref[...] + jnp.dot(mix, wout_ref[...], preferred_element_type=F32)
    h_ref[...] = h
    hn_ref[...] = _rms(h, gffn_ref[...]).astype(BF16)


def combine(ya, ob, pg, x, w_o_a, w_o_b, w_out, g_ffn, tm):
    n = x.shape[0]
    row = lambda w: pl.BlockSpec((tm, w), lambda i: (i, 0))
    res = lambda a: pl.BlockSpec(a.shape, lambda i: (0, 0), pipeline_mode=pl.Buffered(1))
    return pl.pallas_call(
        _combine_body,
        grid=(n // tm,),
        in_specs=[row(RW), row(MLA_HEADS * V_HEAD), row(2 * D_MODEL), row(D_MODEL),
                  res(w_o_a), res(w_o_b), res(w_out), res(g_ffn)],
        out_specs=[row(D_MODEL), row(D_MODEL)],
        out_shape=[jax.ShapeDtypeStruct((n, D_MODEL), F32), jax.ShapeDtypeStruct((n, D_MODEL), BF16)],
        compiler_params=_cparams(("parallel",)),
        name="combine",
    )(ya, ob, pg, x, w_o_a, w_o_b, w_out, g_ffn)


HALO = 16


def _ffn_in_body(hn_ref, halo_ref, wg_ref, wu_ref, cwb_ref, st_ref, act_ref, cnew_ref, *, tm, tiles_per_seq):
    i = pl.program_id(1)
    hn = hn_ref[...]
    gate = jnp.dot(hn, wg_ref[...], preferred_element_type=F32)
    up = jnp.dot(hn, wu_ref[...], preferred_element_type=F32)
    ghalo = jnp.dot(halo_ref[...], wg_ref[...], preferred_element_type=F32)
    first = (i % tiles_per_seq) == 0
    prev2 = jnp.where(first, st_ref[0, 0:1, :], ghalo[HALO - 2:HALO - 1, :])
    prev1 = jnp.where(first, st_ref[0, 1:2, :], ghalo[HALO - 1:HALO, :])
    row = lax.broadcasted_iota(jnp.int32, (tm, 1), 0)
    g1 = jnp.where(row == 0, prev1, pltpu.roll(gate, 1, 0))
    g2 = jnp.where(row == 0, prev2, jnp.where(row == 1, prev1, pltpu.roll(gate, 2, 0)))
    gate_c = g2 * cwb_ref[0:1, :] + g1 * cwb_ref[1:2, :] + gate * cwb_ref[2:3, :] + cwb_ref[3:4, :]
    act_ref[...] = (jax.nn.silu(gate_c) * up).astype(act_ref.dtype)

    @pl.when((i % tiles_per_seq) == tiles_per_seq - 1)
    def _():
        cnew_ref[0] = gate[tm - (CONV_W - 1):tm, :]


def ffn_in(hn, w_ffn_in, cwb, conv_state, bsz, t, tm, tn):
    n = bsz * t
    tps = t // tm
    ncol = D_FF // tn
    r = tm // HALO
    return pl.pallas_call(
        functools.partial(_ffn_in_body, tm=tm, tiles_per_seq=tps),
        grid=(ncol, n // tm),
        in_specs=[pl.BlockSpec((tm, D_MODEL), lambda j, i: (i, 0)),
                  pl.BlockSpec((HALO, D_MODEL), lambda j, i: (jnp.maximum(i * r - 1, 0), 0)),
                  pl.BlockSpec((D_MODEL, tn), lambda j, i: (0, j)),
                  pl.BlockSpec((D_MODEL, tn), lambda j, i: (0, j + ncol)),
                  pl.BlockSpec((8, tn), lambda j, i: (0, j)),
                  pl.BlockSpec((1, 8, tn), lambda j, i: (i // tps, 0, j))],
        out_specs=[pl.BlockSpec((tm, tn), lambda j, i: (i, j)),
                   pl.BlockSpec((1, CONV_W - 1, tn), lambda j, i: (i // tps, 0, j))],
        out_shape=[jax.ShapeDtypeStruct((n, D_FF), BF16),
                   jax.ShapeDtypeStruct((bsz, CONV_W - 1, D_FF), F32)],
        compiler_params=_cparams(("parallel", "arbitrary")),
        name="ffn_in",
    )(hn, hn, w_ffn_in, w_ffn_in, cwb, conv_state)


def _ffn_down_body(act_ref, w_ref, h_ref, g_ref, y_ref, acc_ref):
    k = pl.program_id(1)

    @pl.when(k == 0)
    def _():
        acc_ref[...] = h_ref[...]

    acc_ref[...] += jnp.dot(act_ref[...], w_ref[...], preferred_element_type=F32)

    @pl.when(k == pl.num_programs(1) - 1)
    def _():
        y_ref[...] = _rms(acc_ref[...], g_ref[...])


def ffn_down(act, w_down, h, g_final, tm, tk):
    n = h.shape[0]
    return pl.pallas_call(
        _ffn_down_body,
        grid=(n // tm, D_FF // tk),
        in_specs=[pl.BlockSpec((tm, tk), lambda i, k: (i, k)),
                  pl.BlockSpec((tk, D_MODEL), lambda i, k: (k, 0)),
                  pl.BlockSpec((tm, D_MODEL), lambda i, k: (i, 0)),
                  pl.BlockSpec((1, D_MODEL), lambda i, k: (0, 0))],
        out_specs=pl.BlockSpec((tm, D_MODEL), lambda i, k: (i, 0)),
        out_shape=jax.ShapeDtypeStruct((n, D_MODEL), F32),
        scratch_shapes=[pltpu.VMEM((tm, D_MODEL), F32)],
        compiler_params=_cparams(("parallel", "arbitrary")),
        name="ffn_down",
    )(act, w_down, h, g_final)


def _pad_cols(w, width):
    return jnp.pad(w, ((0, 0), (0, width - w.shape[1])))


def _prepare_params(g_mix, w_in, mu_shift, w0, w2, a0, a2, g2, k_k, k_a, r_k, lnx_g, lnx_b, w_o_a, g_kv,
                    w_ukv, w_o_b, w_out, g_ffn, w_ffn_in, conv_w, conv_b, w_ffn_down, g_final):
    o_zw = 3 * RW
    o_za = o_zw + DECAY_LORA
    o_zg = o_za + AAA_LORA
    o_q = RWKV_PROJ
    o_ckv = o_q + MLA_Q_WIDTH
    o_kr = o_ckv + KV_LORA
    o_g = RWKV_PROJ + MLA_PROJ

    def sect_a(m):
        return jnp.concatenate([m[:, :o_zw], _pad_cols(m[:, o_zw:o_za], LORA_PAD),
                                _pad_cols(m[:, o_za:o_zg], LORA_PAD), m[:, o_zg:RWKV_PROJ]], axis=1)

    wq = w_in[:, o_q:o_ckv].reshape(D_MODEL, MLA_HEADS, QK_NOPE + QK_ROPE)
    w_b = jnp.concatenate([wq[:, :, :QK_NOPE].reshape(D_MODEL, -1), wq[:, :, QK_NOPE:].reshape(D_MODEL, -1),
                           w_in[:, o_ckv:o_kr], _pad_cols(w_in[:, o_kr:o_g], LANE)], axis=1)
    wkv = w_ukv.reshape(KV_LORA, MLA_HEADS, QK_NOPE + V_HEAD)
    w_ukv_p = jnp.concatenate([wkv[:, :, :QK_NOPE].reshape(KV_LORA, -1),
                               wkv[:, :, QK_NOPE:].reshape(KV_LORA, -1)], axis=1)
    pad_rows = lambda m: jnp.pad(m, ((0, LORA_PAD - m.shape[0]), (0, 0)))
    row = lambda vct: vct.reshape(1, -1)
    cwb = jnp.concatenate([conv_w, conv_b.reshape(1, D_FF), jnp.zeros((8 - CONV_W - 1, D_FF), F32)], axis=0)
    return dict(
        g_mix=row(g_mix), w_a=sect_a(w_in[:, :RWKV_PROJ]).astype(BF16), w_b=w_b.astype(BF16),
        w_g=w_in[:, o_g:].astype(BF16), mu_a=sect_a(row(mu_shift)),
        w0=row(w0), w2=pad_rows(w2).astype(BF16), a0=row(a0), a2=pad_rows(a2).astype(BF16),
        g2=g2.astype(BF16), k_k=row(k_k), k_a=row(k_a), r_k=row(r_k.reshape(-1)),
        lnx_g=row(lnx_g), lnx_b=row(lnx_b), w_o_a=w_o_a.astype(BF16), g_kv=row(g_kv),
        w_ukv=w_ukv_p.astype(BF16), w_o_b=w_o_b.astype(BF16), w_out=w_out.astype(BF16),
        g_ffn=row(g_ffn), w_ffn_in=w_ffn_in.astype(BF16), cwb=cwb, w_ffn_down=w_ffn_down.astype(BF16),
        g_final=row(g_final), sect_a=sect_a)


def _rope_tables(pos):
    half = QK_ROPE // 2
    inv = ROPE_THETA ** (-jnp.arange(half, dtype=F32) / half)
    ang = pos.astype(F32)[:, None] * inv[None, :]
    cos, sin = jnp.cos(ang), jnp.sin(ang)
    return jnp.tile(cos, (1, 4)), jnp.tile(jnp.concatenate([-sin, sin], axis=1), (1, 2))


def _tile_rows(n, pref):
    return pref if n % pref == 0 else n


def _layer(x, pos, shift_prev, wkv0, conv_prev, past, prm):
    bsz, t, _ = x.shape
    n = bsz * t
    x2 = x.reshape(n, D_MODEL)
    tm = _tile_rows(n, 512)

    pa = norm_matmul(x2, prm["g_mix"], prm["w_a"], tm, 896)
    pg = norm_matmul(x2, prm["g_mix"], prm["w_g"], tm, 1024)

    shift_p = jnp.pad(prm["sect_a"](shift_prev.reshape(bsz, RWKV_PROJ)).reshape(bsz, 1, PA_W),
                      ((0, 0), (0, 7), (0, 0)))
    ya, wkv_new = rwkv_mix(pa, shift_p, wkv0, prm, bsz, t, min(t, 64))
    last = pa.reshape(bsz, t, PA_W)[:, t - 1]
    o_zw = 3 * RW
    shift_new = jnp.concatenate([last[:, :o_zw], last[:, o_zw:o_zw + DECAY_LORA],
                                 last[:, o_zw + LORA_PAD:o_zw + LORA_PAD + AAA_LORA],
                                 last[:, o_zw + 2 * LORA_PAD:]], axis=1).reshape(bsz, 1, RWKV_PROJ)

    cs, sn = _rope_tables(pos)
    if past is None:
        tmb = _tile_rows(t, 256)
    else:
        tmb = n
        cs, sn = jnp.tile(cs, (bsz, 1)), jnp.tile(sn, (bsz, 1))
    qcat, lat, kr, kcat, v = mla_prep(x2, prm["g_mix"], prm["w_b"], cs, sn, prm["g_kv"], prm["w_ukv"], tmb)
    if past is None:
        ob = attention_prompt(qcat, kcat, v, bsz, t, 256)
    else:
        plat, pkr = past
        plen = plat.shape[1]
        kcat_p, v_p = mla_expand(plat.reshape(bsz * plen, KV_LORA), pkr.reshape(bsz * plen, QK_ROPE),
                                 prm["w_ukv"], 512)
        ob = attention_sample(qcat, kcat_p, v_p, kcat, v, bsz, t, plen)

    h, hn = combine(ya, ob, pg, x2, prm["w_o_a"], prm["w_o_b"], prm["w_out"], prm["g_ffn"],
                    _tile_rows(n, 256))
    conv_p = jnp.pad(conv_prev, ((0, 0), (0, 8 - (CONV_W - 1)), (0, 0)))
    act, conv_new = ffn_in(hn, prm["w_ffn_in"], prm["cwb"], conv_p, bsz, t, min(t, 512), 512)
    y = ffn_down(act, prm["w_ffn_down"], h, prm["g_final"], tm, 512)
    return (y.reshape(bsz, t, D_MODEL), lat.reshape(bsz, t, KV_LORA), kr.reshape(bsz, t, QK_ROPE),
            wkv_new, shift_new, conv_new)


def kernel(x_prompt, x_sample, cache_mla_latent, cache_mla_krope, state_rwkv_wkv, state_rwkv_shift, state_ffn_conv, g_mix, w_in, mu_shift, w0, w2, a0, a2, g2, k_k, k_a, r_k, lnx_g, lnx_b, w_o_a, g_kv, w_ukv, w_o_b, w_out, g_ffn, w_ffn_in, conv_w, conv_b, w_ffn_down, g_final):
    depth = w_in.shape[0]
    assert depth == 1, "single-layer step"
    bp, tp, _ = x_prompt.shape
    ts = x_sample.shape[1]
    past = cache_mla_latent.shape[2]
    prm = _prepare_params(g_mix[0], w_in[0], mu_shift[0], w0[0], w2[0], a0[0], a2[0], g2[0], k_k[0], k_a[0],
                          r_k[0], lnx_g[0], lnx_b[0], w_o_a[0], g_kv[0], w_ukv[0], w_o_b[0], w_out[0],
                          g_ffn[0], w_ffn_in[0], conv_w[0], conv_b[0], w_ffn_down[0], g_final)
    dt = x_prompt.dtype
    out_p = _layer(x_prompt, jnp.arange(tp), jnp.zeros((bp, 1, RWKV_PROJ), dt),
                   jnp.zeros((bp, N_HEADS, HEAD, HEAD), F32), jnp.zeros((bp, CONV_W - 1, D_FF), dt),
                   None, prm)
    out_s = _layer(x_sample, past + jnp.arange(ts), state_rwkv_shift[0], state_rwkv_wkv[0],
                   state_ffn_conv[0], (cache_mla_latent[0], cache_mla_krope[0]), prm)
    lead = lambda a: a[None].astype(dt)
    return (out_p[0], out_s[0],
            lead(out_p[1]), lead(out_p[2]), lead(out_p[3]), lead(out_p[4]), lead(out_p[5]),
            lead(out_s[1]), lead(out_s[2]), lead(out_s[3]), lead(out_s[4]), lead(out_s[5]))
```

```python
import functools
import math

import jax
import jax.numpy as jnp
from jax import lax
from jax.experimental import pallas as pl
from jax.experimental.pallas import tpu as pltpu

F32 = jnp.float32
BF16 = jnp.bfloat16

D_MODEL = 2048
NORM_EPS = 1e-6
HEAD = 64
N_HEADS = 16
RW = N_HEADS * HEAD
DECAY_LORA = 96
AAA_LORA = 96
GATE_LORA = 256
GN_EPS = 64e-5
RWKV_PROJ = 3 * RW + DECAY_LORA + AAA_LORA + GATE_LORA
MLA_HEADS = 8
QK_NOPE = 128
QK_ROPE = 64
V_HEAD = 128
KV_LORA = 512
ROPE_THETA = 10000.0
MLA_Q_WIDTH = MLA_HEADS * (QK_NOPE + QK_ROPE)
MLA_PROJ = MLA_Q_WIDTH + KV_LORA + QK_ROPE
SOFTMAX_SCALE = (QK_NOPE + QK_ROPE) ** -0.5
Q_SCALE = SOFTMAX_SCALE * math.log2(math.e)
NEG_INF = -1e30
CHUNK = 64
D_FF = 5632
CONV_W = 3

LANE = 128
LORA_PAD = 128
PA_W = 3 * RW + 2 * LORA_PAD + GATE_LORA
QR_W = MLA_HEADS * QK_ROPE
PB_W = MLA_HEADS * QK_NOPE + QR_W + KV_LORA + LANE
QK_CAT = 2 * LANE
VMEM_LIMIT = 56 * 1024 * 1024


def _cparams(sem, vmem=VMEM_LIMIT):
    return pltpu.CompilerParams(dimension_semantics=sem, vmem_limit_bytes=vmem)


def _dot(a, b):
    return jnp.dot(a.astype(BF16), b.astype(BF16), preferred_element_type=F32)


def _dot_nt(a, b):
    return lax.dot_general(a.astype(BF16), b.astype(BF16), (((1,), (1,)), ((), ())),
                           preferred_element_type=F32)


def _dot_tn(a, b):
    return lax.dot_general(a.astype(BF16), b.astype(BF16), (((0,), (0,)), ((), ())),
                           preferred_element_type=F32)


def _rms(x, g):
    return x * lax.rsqrt(jnp.mean(x * x, axis=-1, keepdims=True) + NORM_EPS) * g


def _resident(a):
    nd = a.ndim
    return pl.BlockSpec(a.shape, lambda *_: (0,) * nd, pipeline_mode=pl.Buffered(1))


def _norm_matmul_body(x_ref, g_ref, w_ref, o_ref, *, parts):
    rows = x_ref.shape[0] // parts
    for p in range(parts):
        rs = slice(p * rows, (p + 1) * rows)
        xn = _rms(x_ref[rs, :], g_ref[...]).astype(BF16)
        o_ref[rs, :] = jnp.dot(xn, w_ref[...], preferred_element_type=F32).astype(o_ref.dtype)


def norm_matmul(x, g, w, tm, out_dtype=F32):
    n, k = x.shape
    m = w.shape[1]
    return pl.pallas_call(
        functools.partial(_norm_matmul_body, parts=2),
        grid=(n // tm,),
        in_specs=[pl.BlockSpec((tm, k), lambda i: (i, 0)), _resident(g), _resident(w)],
        out_specs=pl.BlockSpec((tm, m), lambda i: (i, 0)),
        out_shape=jax.ShapeDtypeStruct((n, m), out_dtype),
        compiler_params=_cparams(("parallel",)),
        name="norm_matmul",
    )(x, g, w)


def _rwkv_body(pa_ref, shift_ref, wkv0_ref, mu_ref, w0_ref, w2_ref, a0_ref, a2_ref, g2_ref,
               kk_ref, ka_ref, rk_ref, lng_ref, lnb_ref, y_ref, wkv_ref, st_ref, prev_ref, *, c, nch):
    ci = pl.program_id(1)
    nc = pl.num_programs(1)
    rows = nch * c

    @pl.when(ci == 0)
    def _():
        prev_ref[...] = shift_ref[0]
        st_ref[...] = jnp.zeros(st_ref.shape, F32)
        for h in range(N_HEADS):
            o = (h % 2) * HEAD
            st_ref[h // 2, o:o + HEAD, o:o + HEAD] = wkv0_ref[0, h].T

    pa = pa_ref[...]
    row = lax.broadcasted_iota(jnp.int32, (rows, 1), 0)
    p_prev = jnp.where(row == 0, prev_ref[0:1, :], pltpu.roll(pa, 1, 0))
    prev_ref[0:1, :] = pa[rows - 1:rows, :]
    z = pa + mu_ref[...] * (p_prev - pa)

    r_all = z[:, 0:RW]
    k_all = z[:, RW:2 * RW]
    v_all = z[:, 2 * RW:3 * RW]
    zw = z[:, 3 * RW:3 * RW + LORA_PAD]
    za = z[:, 3 * RW + LORA_PAD:3 * RW + 2 * LORA_PAD]
    zg = z[:, 3 * RW + 2 * LORA_PAD:PA_W]

    w_raw = w0_ref[...] + _dot(jnp.tanh(zw), w2_ref[...])
    logw = -jnp.exp(-jax.nn.softplus(-w_raw) - 0.5)
    a_all = jax.nn.sigmoid(a0_ref[...] + _dot(za, a2_ref[...]))
    g_all = _dot(jax.nn.sigmoid(zg), g2_ref[...])
    kk_all = k_all * kk_ref[...]
    k2_all = k_all * (1.0 + (a_all - 1.0) * ka_ref[...])

    ti = lax.broadcasted_iota(jnp.int32, (rows, rows), 0)
    tj = lax.broadcasted_iota(jnp.int32, (rows, rows), 1)
    tri = ((ti >= tj) & (ti // c == tj // c)).astype(BF16)
    w_hi = logw.astype(BF16)
    w_mid = (logw - w_hi.astype(F32)).astype(BF16)
    w_lo = (logw - w_hi.astype(F32) - w_mid.astype(F32)).astype(BF16)
    cum_all = (jnp.dot(tri, w_hi, preferred_element_type=F32) + jnp.dot(tri, w_mid, preferred_element_type=F32)
               + jnp.dot(tri, w_lo, preferred_element_type=F32))

    c2 = 2 * c
    low = lax.broadcasted_iota(jnp.int32, (c, LANE), 1) < HEAD
    ri = lax.broadcasted_iota(jnp.int32, (c2, c2), 0)
    cj = lax.broadcasted_iota(jnp.int32, (c2, c2), 1)
    same = (ri // c) == (cj // c)
    strict = same & (ri > cj)
    incl = same & (ri >= cj)
    eye_p = (lax.broadcasted_iota(jnp.int32, (LANE, LANE), 0)
             == lax.broadcasted_iota(jnp.int32, (LANE, LANE), 1))

    def stack(x):
        return jnp.concatenate([jnp.where(low, x, 0.0), jnp.where(low, 0.0, x)], axis=0).astype(BF16)

    def head_sum(x):
        s_a = jnp.sum(jnp.where(low, x, 0.0), axis=-1, keepdims=True)
        s_b = jnp.sum(jnp.where(low, 0.0, x), axis=-1, keepdims=True)
        return jnp.where(low, s_a, s_b)

    npair = N_HEADS // 2
    items = [(q, p) for q in range(nch) for p in range(npair)]
    idx = range(len(items))
    blk = lambda arr, q, p: arr[q * c:(q + 1) * c, p * LANE:(p + 1) * LANE]
    cum = [blk(cum_all, q, p) for q, p in items]
    lw = [blk(logw, q, p) for q, p in items]
    r = [blk(r_all, q, p) for q, p in items]
    k2 = [blk(k2_all, q, p) for q, p in items]
    v = [blk(v_all, q, p) for q, p in items]
    kk = [blk(kk_all, q, p) for q, p in items]
    kk = [x * lax.rsqrt(jnp.maximum(head_sum(x * x), 1e-12)) for x in kk]
    gam = [jnp.exp(x) for x in cum]
    inv_gam = [jnp.exp(-x) for x in cum]
    gam_c = [jnp.exp(x[c - 1:c, :]) for x in cum]
    at = [-kk[i] * jnp.exp(cum[i] - lw[i]) for i in idx]
    bt = [kk[i] * blk(a_all, *items[i]) * inv_gam[i] for i in idx]
    kt = [k2[i] * inv_gam[i] for i in idx]
    rt = [r[i] * gam[i] for i in idx]
    s_at = [stack(x) for x in at]
    s_rt = [stack(x) for x in rt]
    s_v = [stack(x) for x in v]
    pm = [lax.dot_general(jnp.concatenate([s_at[i], s_rt[i]], axis=0),
                          jnp.concatenate([stack(bt[i]), stack(kt[i])], axis=0),
                          (((1,), (1,)), ((), ())), preferred_element_type=F32) for i in idx]
    a_ab = [jnp.where(strict, x[0:c2, 0:c2], 0.0) for x in pm]
    a_ak = [jnp.where(strict, x[0:c2, c2:2 * c2], 0.0).astype(BF16) for x in pm]
    a_br = [jnp.where(incl, x[c2:2 * c2, 0:c2], 0.0).astype(BF16) for x in pm]
    a_kr = [jnp.where(incl, x[c2:2 * c2, c2:2 * c2], 0.0).astype(BF16) for x in pm]

    xinv = [jnp.where(ri == cj, 1.0, x) for x in a_ab]
    pw = a_ab
    for _ in range(int(math.log2(c)) - 1):
        pw = [_dot(x, x) for x in pw]
        xinv = [xinv[i] + _dot(xinv[i], pw[i]) for i in idx]

    akv = [jnp.dot(a_ak[i], s_v[i], preferred_element_type=F32) for i in idx]
    sol = [_dot(xinv[i], jnp.concatenate([s_at[i], akv[i].astype(BF16)], axis=1)) for i in idx]
    sol_bf = [x.astype(BF16) for x in sol]
    asol = [jnp.dot(a_br[i], sol_bf[i], preferred_element_type=F32) for i in idx]
    akrv = [jnp.dot(a_kr[i], s_v[i], preferred_element_type=F32) for i in idx]
    mn = [_dot_tn(stack(bt[i] * gam_c[i]), sol_bf[i]) for i in idx]
    kgv = [_dot_tn(stack(kt[i] * gam_c[i]), s_v[i]) for i in idx]
    rh = [(s_rt[i] + asol[i][:, 0:LANE]).astype(BF16) for i in idx]
    m_mat = [(jnp.where(eye_p, gam_c[i], 0.0) + mn[i][:, 0:LANE]).astype(BF16) for i in idx]
    n_mat = [mn[i][:, LANE:2 * LANE] + kgv[i] for i in idx]
    y_add = [asol[i][:, LANE:2 * LANE] + akrv[i] for i in idx]

    st = [st_ref[p] for p in range(npair)]
    y_st = []
    for q in range(nch):
        st_bf = [x.astype(BF16) for x in st]
        y_st += [jnp.dot(rh[q * npair + p], st_bf[p], preferred_element_type=F32) + y_add[q * npair + p]
                 for p in range(npair)]
        st = [jnp.dot(m_mat[q * npair + p], st_bf[p], preferred_element_type=F32) + n_mat[q * npair + p]
              for p in range(npair)]
    for p in range(npair):
        st_ref[p] = st[p]

    outs = []
    for i in idx:
        q, p = items[i]
        ps = slice(p * LANE, (p + 1) * LANE)
        y = y_st[i][0:c, :] + y_st[i][c:c2, :]
        yc = y - head_sum(y) * (1.0 / HEAD)
        var = head_sum(yc * yc) * (1.0 / HEAD)
        yn = yc * lax.rsqrt(var + GN_EPS) * lng_ref[:, ps] + lnb_ref[:, ps]
        bonus = head_sum(r[i] * k2[i] * rk_ref[:, ps]) * v[i]
        outs.append((yn + bonus) * blk(g_all, q, p))
    y_ref[...] = jnp.concatenate(
        [jnp.concatenate(outs[q * npair:(q + 1) * npair], axis=1) for q in range(nch)], axis=0).astype(y_ref.dtype)

    @pl.when(ci == nc - 1)
    def _():
        for h in range(N_HEADS):
            o = (h % 2) * HEAD
            wkv_ref[0, h] = st_ref[h // 2, o:o + HEAD, o:o + HEAD].T


def rwkv_mix(pa, shift_prev, wkv0, prm, bsz, t, c, nch):
    rows = c * nch
    nc = t // rows
    vec = lambda w: pl.BlockSpec((1, w), lambda b, i: (0, 0))
    mat = lambda r, w: pl.BlockSpec((r, w), lambda b, i: (0, 0))
    return pl.pallas_call(
        functools.partial(_rwkv_body, c=c, nch=nch),
        grid=(bsz, nc),
        in_specs=[pl.BlockSpec((rows, PA_W), lambda b, i: (b * nc + i, 0)),
                  pl.BlockSpec((1, 8, PA_W), lambda b, i: (b, 0, 0)),
                  pl.BlockSpec((1, N_HEADS, HEAD, HEAD), lambda b, i: (b, 0, 0, 0)),
                  vec(PA_W), vec(RW), mat(LORA_PAD, RW), vec(RW), mat(LORA_PAD, RW), mat(GATE_LORA, RW),
                  vec(RW), vec(RW), vec(RW), vec(RW), vec(RW)],
        out_specs=[pl.BlockSpec((rows, RW), lambda b, i: (b * nc + i, 0)),
                   pl.BlockSpec((1, N_HEADS, HEAD, HEAD), lambda b, i: (b, 0, 0, 0))],
        out_shape=[jax.ShapeDtypeStruct((bsz * t, RW), BF16),
                   jax.ShapeDtypeStruct((bsz, N_HEADS, HEAD, HEAD), F32)],
        scratch_shapes=[pltpu.VMEM((N_HEADS // 2, LANE, LANE), F32), pltpu.VMEM((8, PA_W), F32)],
        compiler_params=_cparams(("parallel", "arbitrary")),
        name="rwkv_mix",
    )(pa, shift_prev, wkv0, prm["mu_a"], prm["w0"], prm["w2"], prm["a0"], prm["a2"], prm["g2"],
      prm["k_k"], prm["k_a"], prm["r_k"], prm["lnx_g"], prm["lnx_b"])


def _rope_swap(slab):
    lane = lax.broadcasted_iota(jnp.int32, slab.shape, 1)
    return jnp.where(lane % QK_ROPE < QK_ROPE // 2, pltpu.roll(slab, LANE - QK_ROPE // 2, 1),
                     pltpu.roll(slab, QK_ROPE // 2, 1))


def _expand_store(lat_bf, kslab_bf, wukv_ref, kcat_ref, v_ref):
    kv = jnp.dot(lat_bf, wukv_ref[...], preferred_element_type=F32)
    for h in range(MLA_HEADS):
        kcat_ref[:, h * QK_CAT:h * QK_CAT + LANE] = kv[:, h * QK_NOPE:(h + 1) * QK_NOPE].astype(BF16)
        kcat_ref[:, h * QK_CAT + LANE:(h + 1) * QK_CAT] = kslab_bf
    v_ref[...] = kv[:, MLA_HEADS * QK_NOPE:].astype(BF16)


def _mla_prep_body(x_ref, g_ref, wb_ref, cs_ref, sn_ref, gkv_ref, wukv_ref,
                   qcat_ref, lat_ref, kr_ref, kcat_ref, v_ref):
    xn = _rms(x_ref[...], g_ref[...]).astype(BF16)
    pb = jnp.dot(xn, wb_ref[...], preferred_element_type=F32)
    cs = cs_ref[...]
    sn = sn_ref[...]
    lane = lax.broadcasted_iota(jnp.int32, cs.shape, 1)
    low = lane < QK_ROPE
    nq = MLA_HEADS * QK_NOPE
    for p in range(MLA_HEADS // 2):
        slab = pb[:, nq + p * LANE:nq + (p + 1) * LANE]
        roped = (slab * cs + _rope_swap(slab) * sn) * Q_SCALE
        for j in range(2):
            h = 2 * p + j
            qn = pb[:, h * QK_NOPE:(h + 1) * QK_NOPE] * Q_SCALE
            qcat_ref[:, h * QK_CAT:h * QK_CAT + LANE] = qn.astype(BF16)
            half = roped if j == 0 else pltpu.roll(roped, QK_ROPE, 1)
            qcat_ref[:, h * QK_CAT + LANE:(h + 1) * QK_CAT] = jnp.where(low, half, 0.0).astype(BF16)
    lat = _rms(pb[:, nq + QR_W:nq + QR_W + KV_LORA], gkv_ref[...])
    lat_ref[...] = lat
    kslab = pb[:, nq + QR_W + KV_LORA:PB_W]
    kroped = jnp.where(low, kslab * cs + _rope_swap(kslab) * sn, 0.0)
    kr_ref[...] = kroped[:, 0:QK_ROPE]
    _expand_store(lat.astype(BF16), kroped.astype(BF16), wukv_ref, kcat_ref, v_ref)


def mla_prep(x, g_mix, w_b, cs, sn, g_kv, w_ukv, tm):
    n = x.shape[0]
    ntab = cs.shape[0] // tm
    full = lambda a: pl.BlockSpec(a.shape, lambda i: (0, 0))
    return pl.pallas_call(
        _mla_prep_body,
        grid=(n // tm,),
        in_specs=[pl.BlockSpec((tm, D_MODEL), lambda i: (i, 0)), full(g_mix), full(w_b),
                  pl.BlockSpec((tm, LANE), lambda i: (i % ntab, 0)),
                  pl.BlockSpec((tm, LANE), lambda i: (i % ntab, 0)), full(g_kv), full(w_ukv)],
        out_specs=[pl.BlockSpec((tm, MLA_HEADS * QK_CAT), lambda i: (i, 0)),
                   pl.BlockSpec((tm, KV_LORA), lambda i: (i, 0)),
                   pl.BlockSpec((tm, QK_ROPE), lambda i: (i, 0)),
                   pl.BlockSpec((tm, MLA_HEADS * QK_CAT), lambda i: (i, 0)),
                   pl.BlockSpec((tm, MLA_HEADS * V_HEAD), lambda i: (i, 0))],
        out_shape=[jax.ShapeDtypeStruct((n, MLA_HEADS * QK_CAT), BF16),
                   jax.ShapeDtypeStruct((n, KV_LORA), F32),
                   jax.ShapeDtypeStruct((n, QK_ROPE), F32),
                   jax.ShapeDtypeStruct((n, MLA_HEADS * QK_CAT), BF16),
                   jax.ShapeDtypeStruct((n, MLA_HEADS * V_HEAD), BF16)],
        compiler_params=_cparams(("parallel",)),
        name="mla_prep",
    )(x, g_mix, w_b, cs, sn, g_kv, w_ukv)


def _mla_expand_body(lat_ref, kr_ref, pad_ref, wukv_ref, kcat_ref, v_ref):
    kslab = jnp.dot(kr_ref[...].astype(BF16), pad_ref[...], preferred_element_type=F32).astype(BF16)
    _expand_store(lat_ref[...].astype(BF16), kslab, wukv_ref, kcat_ref, v_ref)


def mla_expand(lat, kr, w_ukv, tm):
    n = lat.shape[0]
    pad = jnp.eye(QK_ROPE, LANE, dtype=BF16)
    return pl.pallas_call(
        _mla_expand_body,
        grid=(n // tm,),
        in_specs=[pl.BlockSpec((tm, KV_LORA), lambda i: (i, 0)),
                  pl.BlockSpec((tm, QK_ROPE), lambda i: (i, 0)),
                  pl.BlockSpec(pad.shape, lambda i: (0, 0)),
                  pl.BlockSpec(w_ukv.shape, lambda i: (0, 0))],
        out_specs=[pl.BlockSpec((tm, MLA_HEADS * QK_CAT), lambda i: (i, 0)),
                   pl.BlockSpec((tm, MLA_HEADS * V_HEAD), lambda i: (i, 0))],
        out_shape=[jax.ShapeDtypeStruct((n, MLA_HEADS * QK_CAT), BF16),
                   jax.ShapeDtypeStruct((n, MLA_HEADS * V_HEAD), BF16)],
        compiler_params=_cparams(("parallel",)),
        name="mla_expand",
    )(lat, kr, pad, w_ukv)


def _softmax_block(s, v, m, l, acc):
    m_new = jnp.maximum(m, jnp.max(s, axis=-1, keepdims=True))
    alpha = jnp.exp2(m - m_new)
    p = jnp.exp2(s - m_new)
    l_new = alpha * l + jnp.sum(p, axis=-1, keepdims=True)
    acc_new = alpha * acc + jnp.dot(p.astype(BF16), v, preferred_element_type=F32)
    return m_new, l_new, acc_new


def _scores(q, kblk):
    return lax.dot_general(q, kblk, (((1,), (1,)), ((), ())), preferred_element_type=F32)


def _qk_cols(h):
    return slice(h * QK_CAT, (h + 1) * QK_CAT)


def _v_cols(h):
    return slice(h * V_HEAD, (h + 1) * V_HEAD)


def _attn_prompt_body(q_ref, k_ref, v_ref, o_ref, *, tq, hb):
    i = pl.program_id(2)
    start = pl.multiple_of(i * tq, tq)
    qc = lax.broadcasted_iota(jnp.int32, (tq, tq), 0) // CHUNK
    kc = lax.broadcasted_iota(jnp.int32, (tq, tq), 1) // CHUNK
    visible = kc <= qc
    init = []
    for h in range(hb):
        s = jnp.where(visible, _scores(q_ref[:, _qk_cols(h)], k_ref[pl.ds(start, tq), _qk_cols(h)]), NEG_INF)
        m = jnp.max(s, axis=-1, keepdims=True)
        p = jnp.exp2(s - m)
        init.append((m, jnp.sum(p, axis=-1, keepdims=True),
                     jnp.dot(p.astype(BF16), v_ref[pl.ds(start, tq), _v_cols(h)], preferred_element_type=F32)))

    def body(j, carry):
        off = pl.multiple_of(j * tq, tq)
        return tuple(_softmax_block(_scores(q_ref[:, _qk_cols(h)], k_ref[pl.ds(off, tq), _qk_cols(h)]),
                                    v_ref[pl.ds(off, tq), _v_cols(h)], *carry[h]) for h in range(hb))

    final = lax.fori_loop(0, i, body, tuple(init))
    for h in range(hb):
        m, l, acc = final[h]
        o_ref[:, _v_cols(h)] = (acc / l).astype(o_ref.dtype)


def attention_prompt(qcat, kcat, v, bsz, t, tq, hb):
    nq = t // tq
    return pl.pallas_call(
        functools.partial(_attn_prompt_body, tq=tq, hb=hb),
        grid=(bsz, MLA_HEADS // hb, nq),
        in_specs=[pl.BlockSpec((tq, hb * QK_CAT), lambda b, h, i: (b * nq + i, h)),
                  pl.BlockSpec((t, hb * QK_CAT), lambda b, h, i: (b, h)),
                  pl.BlockSpec((t, hb * V_HEAD), lambda b, h, i: (b, h))],
        out_specs=pl.BlockSpec((tq, hb * V_HEAD), lambda b, h, i: (b * nq + i, h)),
        out_shape=jax.ShapeDtypeStruct((bsz * t, MLA_HEADS * V_HEAD), BF16),
        compiler_params=_cparams(("parallel", "parallel", "arbitrary")),
        name="attention_prompt",
    )(qcat, kcat, v)


def _attn_sample_body(q_ref, kp_ref, vp_ref, kn_ref, vn_ref, o_ref):
    for h in range(MLA_HEADS):
        q = q_ref[:, _qk_cols(h)]
        s = _scores(q, kp_ref[:, _qk_cols(h)])
        m = jnp.max(s, axis=-1, keepdims=True)
        p = jnp.exp2(s - m)
        l = jnp.sum(p, axis=-1, keepdims=True)
        acc = jnp.dot(p.astype(BF16), vp_ref[:, _v_cols(h)], preferred_element_type=F32)
        m, l, acc = _softmax_block(_scores(q, kn_ref[:, _qk_cols(h)]), vn_ref[:, _v_cols(h)], m, l, acc)
        o_ref[:, _v_cols(h)] = (acc / l).astype(o_ref.dtype)


def attention_sample(qcat, kcat_past, v_past, kcat_new, v_new, bsz, t, past):
    qk_w, v_w = MLA_HEADS * QK_CAT, MLA_HEADS * V_HEAD
    return pl.pallas_call(
        _attn_sample_body,
        grid=(bsz,),
        in_specs=[pl.BlockSpec((t, qk_w), lambda b: (b, 0)),
                  pl.BlockSpec((past, qk_w), lambda b: (b, 0)),
                  pl.BlockSpec((past, v_w), lambda b: (b, 0)),
                  pl.BlockSpec((t, qk_w), lambda b: (b, 0)),
                  pl.BlockSpec((t, v_w), lambda b: (b, 0))],
        out_specs=pl.BlockSpec((t, v_w), lambda b: (b, 0)),
        out_shape=jax.ShapeDtypeStruct((bsz * t, v_w), BF16),
        compiler_params=_cparams(("parallel",)),
        name="attention_sample",
    )(qcat, kcat_past, v_past, kcat_new, v_new)


def _combine_body(ya_ref, ob_ref, pg_ref, x_ref, woa_ref, wob_ref, wout_ref, gffn_ref, h_ref, hn_ref):
    y_a = jnp.dot(ya_ref[...], woa_ref[...], preferred_element_type=F32)
    y_b = jnp.dot(ob_ref[...], wob_ref[...], preferred_element_type=F32)
    gate_a = jax.nn.sigmoid(pg_ref[:, 0:D_MODEL])
    gate_b = jax.nn.sigmoid(pg_ref[:, D_MODEL:2 * D_MODEL])
    mix = (gate_a * y_a + gate_b * y_b).astype(BF16)
    h = x_ref[...] + jnp.dot(mix, wout_ref[...], preferred_element_type=F32)
    h_ref[...] = h
    hn_ref[...] = _rms(h, gffn_ref[...]).astype(BF16)


def combine(ya, ob, pg, x, w_o_a, w_o_b, w_out, g_ffn, tm):
    n = x.shape[0]
    row = lambda w: pl.BlockSpec((tm, w), lambda i: (i, 0))
    res = lambda a: pl.BlockSpec(a.shape, lambda i: (0, 0), pipeline_mode=pl.Buffered(1))
    return pl.pallas_call(
        _combine_body,
        grid=(n // tm,),
        in_specs=[row(RW), row(MLA_HEADS * V_HEAD), row(2 * D_MODEL), row(D_MODEL),
                  res(w_o_a), res(w_o_b), res(w_out), res(g_ffn)],
        out_specs=[row(D_MODEL), row(D_MODEL)],
        out_shape=[jax.ShapeDtypeStruct((n, D_MODEL), F32), jax.ShapeDtypeStruct((n, D_MODEL), BF16)],
        compiler_params=_cparams(("parallel",)),
        name="combine",
    )(ya, ob, pg, x, w_o_a, w_o_b, w_out, g_ffn)


HALO = 16


def _ffn_in_body(hn_ref, halo_ref, wg_ref, wu_ref, cwb_ref, st_ref, act_ref, cnew_ref, *,
                 tm, ts, tiles_per_seq, parts):
    i = pl.program_id(1)
    nseq = tm // ts
    tn = act_ref.shape[1]
    hn = hn_ref[...]
    t_in = lax.broadcasted_iota(jnp.int32, (tm, 1), 0) % ts
    first = (i % tiles_per_seq) == 0
    w = tn // parts
    for p in range(parts):
        cs = slice(p * w, (p + 1) * w)
        gate = jnp.dot(hn, wg_ref[:, cs], preferred_element_type=F32)
        up = jnp.dot(hn, wu_ref[:, cs], preferred_element_type=F32)
        if nseq == 1:
            prev2, prev1 = st_ref[0, 0:1, cs], st_ref[0, 1:2, cs]
        else:
            prev2 = jnp.broadcast_to(st_ref[:, 0:1, cs], (nseq, ts, w)).reshape(tm, w)
            prev1 = jnp.broadcast_to(st_ref[:, 1:2, cs], (nseq, ts, w)).reshape(tm, w)
        if tiles_per_seq > 1:
            ghalo = jnp.dot(halo_ref[...], wg_ref[:, cs], preferred_element_type=F32)
            prev2 = jnp.where(first, prev2, ghalo[HALO - 2:HALO - 1, :])
            prev1 = jnp.where(first, prev1, ghalo[HALO - 1:HALO, :])
        g1 = jnp.where(t_in == 0, prev1, pltpu.roll(gate, 1, 0))
        g2 = jnp.where(t_in == 0, prev2, jnp.where(t_in == 1, prev1, pltpu.roll(gate, 2, 0)))
        gate_c = g2 * cwb_ref[0:1, cs] + g1 * cwb_ref[1:2, cs] + gate * cwb_ref[2:3, cs] + cwb_ref[3:4, cs]
        act_ref[:, cs] = (jax.nn.silu(gate_c) * up).astype(act_ref.dtype)

        @pl.when((i % tiles_per_seq) == tiles_per_seq - 1)
        def _():
            cnew_ref[:, :, cs] = gate.reshape(nseq, ts, w)[:, ts - (CONV_W - 1):ts, :]


def ffn_in(hn, w_ffn_in, cwb, conv_state, bsz, t, tm, tn):
    n = bsz * t
    ts = min(t, tm)
    tps = t // ts
    nseq = tm // ts
    ncol = D_FF // tn
    r = tm // HALO
    return pl.pallas_call(
        functools.partial(_ffn_in_body, tm=tm, ts=ts, tiles_per_seq=tps, parts=1),
        grid=(ncol, n // tm),
        in_specs=[pl.BlockSpec((tm, D_MODEL), lambda j, i: (i, 0)),
                  pl.BlockSpec((HALO, D_MODEL), lambda j, i: (jnp.maximum(i * r - 1, 0), 0)),
                  pl.BlockSpec((D_MODEL, tn), lambda j, i: (0, j)),
                  pl.BlockSpec((D_MODEL, tn), lambda j, i: (0, j + ncol)),
                  pl.BlockSpec((8, tn), lambda j, i: (0, j)),
                  pl.BlockSpec((nseq, 8, tn), lambda j, i: (i // tps, 0, j))],
        out_specs=[pl.BlockSpec((tm, tn), lambda j, i: (i, j)),
                   pl.BlockSpec((nseq, CONV_W - 1, tn), lambda j, i: (i // tps, 0, j))],
        out_shape=[jax.ShapeDtypeStruct((n, D_FF), BF16),
                   jax.ShapeDtypeStruct((bsz, CONV_W - 1, D_FF), F32)],
        compiler_params=_cparams(("parallel", "arbitrary")),
        name="ffn_in",
    )(hn, hn, w_ffn_in, w_ffn_in, cwb, conv_state)


def _ffn_down_body(act_ref, w_ref, h_ref, g_ref, y_ref):
    y_ref[...] = _rms(h_ref[...] + jnp.dot(act_ref[...], w_ref[...], preferred_element_type=F32), g_ref[...])


def ffn_down(act, w_down, h, g_final, tm):
    n = h.shape[0]
    return pl.pallas_call(
        _ffn_down_body,
        grid=(n // tm,),
        in_specs=[pl.BlockSpec((tm, D_FF), lambda i: (i, 0)), _resident(w_down),
                  pl.BlockSpec((tm, D_MODEL), lambda i: (i, 0)), _resident(g_final)],
        out_specs=pl.BlockSpec((tm, D_MODEL), lambda i: (i, 0)),
        out_shape=jax.ShapeDtypeStruct((n, D_MODEL), F32),
        compiler_params=_cparams(("parallel",)),
        name="ffn_down",
    )(act, w_down, h, g_final)


def _pad_cols(w, width):
    return jnp.pad(w, ((0, 0), (0, width - w.shape[1])))


def _prepare_params(g_mix, w_in, mu_shift, w0, w2, a0, a2, g2, k_k, k_a, r_k, lnx_g, lnx_b, w_o_a, g_kv,
                    w_ukv, w_o_b, w_out, g_ffn, w_ffn_in, conv_w, conv_b, w_ffn_down, g_final):
    o_zw = 3 * RW
    o_za = o_zw + DECAY_LORA
    o_zg = o_za + AAA_LORA
    o_q = RWKV_PROJ
    o_ckv = o_q + MLA_Q_WIDTH
    o_kr = o_ckv + KV_LORA
    o_g = RWKV_PROJ + MLA_PROJ

    def sect_a(m):
        return jnp.concatenate([m[:, :o_zw], _pad_cols(m[:, o_zw:o_za], LORA_PAD),
                                _pad_cols(m[:, o_za:o_zg], LORA_PAD), m[:, o_zg:RWKV_PROJ]], axis=1)

    wq = w_in[:, o_q:o_ckv].reshape(D_MODEL, MLA_HEADS, QK_NOPE + QK_ROPE)
    w_b = jnp.concatenate([wq[:, :, :QK_NOPE].reshape(D_MODEL, -1), wq[:, :, QK_NOPE:].reshape(D_MODEL, -1),
                           w_in[:, o_ckv:o_kr], _pad_cols(w_in[:, o_kr:o_g], LANE)], axis=1)
    wkv = w_ukv.reshape(KV_LORA, MLA_HEADS, QK_NOPE + V_HEAD)
    w_ukv_p = jnp.concatenate([wkv[:, :, :QK_NOPE].reshape(KV_LORA, -1),
                               wkv[:, :, QK_NOPE:].reshape(KV_LORA, -1)], axis=1)
    pad_rows = lambda m: jnp.pad(m, ((0, LORA_PAD - m.shape[0]), (0, 0)))
    row = lambda vct: vct.reshape(1, -1)
    cwb = jnp.concatenate([conv_w, conv_b.reshape(1, D_FF), jnp.zeros((8 - CONV_W - 1, D_FF), F32)], axis=0)
    return dict(
        g_mix=row(g_mix), w_a=sect_a(w_in[:, :RWKV_PROJ]).astype(BF16), w_b=w_b.astype(BF16),
        w_g=w_in[:, o_g:].astype(BF16), mu_a=sect_a(row(mu_shift)),
        w0=row(w0), w2=pad_rows(w2).astype(BF16), a0=row(a0), a2=pad_rows(a2).astype(BF16),
        g2=g2.astype(BF16), k_k=row(k_k), k_a=row(k_a), r_k=row(r_k.reshape(-1)),
        lnx_g=row(lnx_g), lnx_b=row(lnx_b), w_o_a=w_o_a.astype(BF16), g_kv=row(g_kv),
        w_ukv=w_ukv_p.astype(BF16), w_o_b=w_o_b.astype(BF16), w_out=w_out.astype(BF16),
        g_ffn=row(g_ffn), w_ffn_in=w_ffn_in.astype(BF16), cwb=cwb, w_ffn_down=w_ffn_down.astype(BF16),
        g_final=row(g_final), sect_a=sect_a)


def _rope_tables(pos):
    half = QK_ROPE // 2
    inv = ROPE_THETA ** (-jnp.arange(half, dtype=F32) / half)
    ang = pos.astype(F32)[:, None] * inv[None, :]
    cos, sin = jnp.cos(ang), jnp.sin(ang)
    return jnp.tile(cos, (1, 4)), jnp.tile(jnp.concatenate([-sin, sin], axis=1), (1, 2))


def _tile_rows(n, pref):
    return pref if n % pref == 0 else n


def _layer(x, pos, shift_prev, wkv0, conv_prev, past, prm):
    bsz, t, _ = x.shape
    n = bsz * t
    x2 = x.reshape(n, D_MODEL)
    tm = _tile_rows(n, 256)

    pa = norm_matmul(x2, prm["g_mix"], prm["w_a"], tm)
    pg = norm_matmul(x2, prm["g_mix"], prm["w_g"], tm)

    shift_p = jnp.pad(prm["sect_a"](shift_prev.reshape(bsz, RWKV_PROJ)).reshape(bsz, 1, PA_W),
                      ((0, 0), (0, 7), (0, 0)))
    ya, wkv_new = rwkv_mix(pa, shift_p, wkv0, prm, bsz, t, min(t, 64), 4 if t >= 256 else 1)
    last = pa.reshape(bsz, t, PA_W)[:, t - 1]
    o_zw = 3 * RW
    shift_new = jnp.concatenate([last[:, :o_zw], last[:, o_zw:o_zw + DECAY_LORA],
                                 last[:, o_zw + LORA_PAD:o_zw + LORA_PAD + AAA_LORA],
                                 last[:, o_zw + 2 * LORA_PAD:]], axis=1).reshape(bsz, 1, RWKV_PROJ)

    cs, sn = _rope_tables(pos)
    if past is None:
        tmb = _tile_rows(t, 256)
    else:
        tmb = n
        cs, sn = jnp.tile(cs, (bsz, 1)), jnp.tile(sn, (bsz, 1))
    qcat, lat, kr, kcat, v = mla_prep(x2, prm["g_mix"], prm["w_b"], cs, sn, prm["g_kv"], prm["w_ukv"], tmb)
    if past is None:
        ob = attention_prompt(qcat, kcat, v, bsz, t, 512, 4)
    else:
        plat, pkr = past
        plen = plat.shape[1]
        kcat_p, v_p = mla_expand(plat.reshape(bsz * plen, KV_LORA), pkr.reshape(bsz * plen, QK_ROPE),
                                 prm["w_ukv"], 512)
        ob = attention_sample(qcat, kcat_p, v_p, kcat, v, bsz, t, plen)

    h, hn = combine(ya, ob, pg, x2, prm["w_o_a"], prm["w_o_b"], prm["w_out"], prm["g_ffn"], tm)
    conv_p = jnp.pad(conv_prev, ((0, 0), (0, 8 - (CONV_W - 1)), (0, 0)))
    act, conv_new = ffn_in(hn, prm["w_ffn_in"], prm["cwb"], conv_p, bsz, t, _tile_rows(n, 512), 512)
    y = ffn_down(act, prm["w_ffn_down"], h, prm["g_final"], tm)
    return (y.reshape(bsz, t, D_MODEL), lat.reshape(bsz, t, KV_LORA), kr.reshape(bsz, t, QK_ROPE),
            wkv_new, shift_new, conv_new)


def kernel(x_prompt, x_sample, cache_mla_latent, cache_mla_krope, state_rwkv_wkv, state_rwkv_shift, state_ffn_conv, g_mix, w_in, mu_shift, w0, w2, a0, a2, g2, k_k, k_a, r_k, lnx_g, lnx_b, w_o_a, g_kv, w_ukv, w_o_b, w_out, g_ffn, w_ffn_in, conv_w, conv_b, w_ffn_down, g_final):
    depth = w_in.shape[0]
    assert depth == 1, "single-layer step"
    bp, tp, _ = x_prompt.shape
    ts = x_sample.shape[1]
    past = cache_mla_latent.shape[2]
    prm = _prepare_params(g_mix[0], w_in[0], mu_shift[0], w0[0], w2[0], a0[0], a2[0], g2[0], k_k[0], k_a[0],
                          r_k[0], lnx_g[0], lnx_b[0], w_o_a[0], g_kv[0], w_ukv[0], w_o_b[0], w_out[0],
                          g_ffn[0], w_ffn_in[0], conv_w[0], conv_b[0], w_ffn_down[0], g_final)
    dt = x_prompt.dtype
    out_p = _layer(x_prompt, jnp.arange(tp), jnp.zeros((bp, 1, RWKV_PROJ), dt),
                   jnp.zeros((bp, N_HEADS, HEAD, HEAD), F32), jnp.zeros((bp, CONV_W - 1, D_FF), dt),
                   None, prm)
    out_s = _layer(x_sample, past + jnp.arange(ts), state_rwkv_shift[0], state_rwkv_wkv[0],
                   state_ffn_conv[0], (cache_mla_latent[0], cache_mla_krope[0]), prm)
    lead = lambda a: a[None].astype(dt)
    return (out_p[0], out_s[0],
            lead(out_p[1]), lead(out_p[2]), lead(out_p[3]), lead(out_p[4]), lead(out_p[5]),
            lead(out_s[1]), lead(out_s[2]), lead(out_s[3]), lead(out_s[4]), lead(out_s[5]))
```

```python
import functools
import math

import jax
import jax.numpy as jnp
from jax import lax
from jax.experimental import pallas as pl
from jax.experimental.pallas import tpu as pltpu

F32 = jnp.float32
BF16 = jnp.bfloat16

D_MODEL = 2048
NORM_EPS = 1e-6
HEAD = 64
N_HEADS = 16
RW = N_HEADS * HEAD
DECAY_LORA = 96
AAA_LORA = 96
GATE_LORA = 256
GN_EPS = 64e-5
RWKV_PROJ = 3 * RW + DECAY_LORA + AAA_LORA + GATE_LORA
MLA_HEADS = 8
QK_NOPE = 128
QK_ROPE = 64
V_HEAD = 128
KV_LORA = 512
ROPE_THETA = 10000.0
MLA_Q_WIDTH = MLA_HEADS * (QK_NOPE + QK_ROPE)
MLA_PROJ = MLA_Q_WIDTH + KV_LORA + QK_ROPE
SOFTMAX_SCALE = (QK_NOPE + QK_ROPE) ** -0.5
Q_SCALE = SOFTMAX_SCALE * math.log2(math.e)
NEG_INF = -1e30
CHUNK = 64
D_FF = 5632
CONV_W = 3

LANE = 128
LORA_PAD = 128
GATE_PIECE = 256
PA_W = 3 * RW + 2 * LORA_PAD + GATE_LORA
QR_W = MLA_HEADS * QK_ROPE
PB_W = MLA_HEADS * QK_NOPE + QR_W + KV_LORA + LANE
QK_CAT = 2 * LANE
VMEM_LIMIT = 56 * 1024 * 1024


def _cparams(sem, vmem=VMEM_LIMIT):
    return pltpu.CompilerParams(dimension_semantics=sem, vmem_limit_bytes=vmem)


def _dot(a, b):
    return jnp.dot(a.astype(BF16), b.astype(BF16), preferred_element_type=F32)


def _dot_nt(a, b):
    return lax.dot_general(a.astype(BF16), b.astype(BF16), (((1,), (1,)), ((), ())),
                           preferred_element_type=F32)


def _dot_tn(a, b):
    return lax.dot_general(a.astype(BF16), b.astype(BF16), (((0,), (0,)), ((), ())),
                           preferred_element_type=F32)


def _rms(x, g):
    return x * lax.rsqrt(jnp.mean(x * x, axis=-1, keepdims=True) + NORM_EPS) * g


def _resident(a):
    nd = a.ndim
    return pl.BlockSpec(a.shape, lambda *_: (0,) * nd, pipeline_mode=pl.Buffered(1))


def _norm_matmul_body(x_ref, g_ref, w_ref, o_ref, *, parts):
    rows = x_ref.shape[0] // parts
    for p in range(parts):
        rs = slice(p * rows, (p + 1) * rows)
        xn = _rms(x_ref[rs, :], g_ref[...]).astype(BF16)
        o_ref[rs, :] = jnp.dot(xn, w_ref[...], preferred_element_type=F32).astype(o_ref.dtype)


def norm_matmul(x, g, w, tm, out_dtype=F32):
    n, k = x.shape
    m = w.shape[1]
    return pl.pallas_call(
        functools.partial(_norm_matmul_body, parts=2),
        grid=(n // tm,),
        in_specs=[pl.BlockSpec((tm, k), lambda i: (i, 0)), _resident(g), _resident(w)],
        out_specs=pl.BlockSpec((tm, m), lambda i: (i, 0)),
        out_shape=jax.ShapeDtypeStruct((n, m), out_dtype),
        compiler_params=_cparams(("parallel",)),
        name="norm_matmul",
    )(x, g, w)


def _rwkv_body(pa_ref, shift_ref, wkv0_ref, mu_ref, w0_ref, w2_ref, a0_ref, a2_ref, g2_ref,
               kk_ref, ka_ref, rk_ref, lng_ref, lnb_ref, x_ref, gmix_ref, wg_ref,
               y_ref, wkv_ref, pg_ref, st_ref, prev_ref, *, c, nch):
    ci = pl.program_id(1)
    nc = pl.num_programs(1)
    rows = nch * c

    @pl.when(ci == 0)
    def _():
        prev_ref[...] = shift_ref[0]
        st_ref[...] = jnp.zeros(st_ref.shape, F32)
        for h in range(N_HEADS):
            o = (h % 2) * HEAD
            st_ref[h // 2, o:o + HEAD, o:o + HEAD] = wkv0_ref[0, h].T

    pa = pa_ref[...]
    row = lax.broadcasted_iota(jnp.int32, (rows, 1), 0)
    p_prev = jnp.where(row == 0, prev_ref[0:1, :], pltpu.roll(pa, 1, 0))
    prev_ref[0:1, :] = pa[rows - 1:rows, :]
    z = pa + mu_ref[...] * (p_prev - pa)

    xn = _rms(x_ref[...], gmix_ref[...]).astype(BF16)
    pieces = iter(range(wg_ref.shape[1] // GATE_PIECE))

    def gate_pieces(k):
        for _ in range(k):
            j = next(pieces, None)
            if j is not None:
                cs = slice(j * GATE_PIECE, (j + 1) * GATE_PIECE)
                pg_ref[:, cs] = jnp.dot(xn, wg_ref[:, cs], preferred_element_type=F32)

    gate_pieces(2)

    r_all = z[:, 0:RW]
    k_all = z[:, RW:2 * RW]
    v_all = z[:, 2 * RW:3 * RW]
    zw = z[:, 3 * RW:3 * RW + LORA_PAD]
    za = z[:, 3 * RW + LORA_PAD:3 * RW + 2 * LORA_PAD]
    zg = z[:, 3 * RW + 2 * LORA_PAD:PA_W]

    w_raw = w0_ref[...] + _dot(jnp.tanh(zw), w2_ref[...])
    logw = -jnp.exp(-jax.nn.softplus(-w_raw) - 0.5)
    a_all = jax.nn.sigmoid(a0_ref[...] + _dot(za, a2_ref[...]))
    g_all = _dot(jax.nn.sigmoid(zg), g2_ref[...])
    gate_pieces(1)
    kk_all = k_all * kk_ref[...]
    k2_all = k_all * (1.0 + (a_all - 1.0) * ka_ref[...])

    ti = lax.broadcasted_iota(jnp.int32, (rows, rows), 0)
    tj = lax.broadcasted_iota(jnp.int32, (rows, rows), 1)
    tri = ((ti >= tj) & (ti // c == tj // c)).astype(BF16)
    w_hi = logw.astype(BF16)
    w_mid = (logw - w_hi.astype(F32)).astype(BF16)
    w_lo = (logw - w_hi.astype(F32) - w_mid.astype(F32)).astype(BF16)
    cum_all = (jnp.dot(tri, w_hi, preferred_element_type=F32) + jnp.dot(tri, w_mid, preferred_element_type=F32)
               + jnp.dot(tri, w_lo, preferred_element_type=F32))

    c2 = 2 * c
    low = lax.broadcasted_iota(jnp.int32, (c, LANE), 1) < HEAD
    ri = lax.broadcasted_iota(jnp.int32, (c2, c2), 0)
    cj = lax.broadcasted_iota(jnp.int32, (c2, c2), 1)
    same = (ri // c) == (cj // c)
    strict = same & (ri > cj)
    incl = same & (ri >= cj)
    eye_p = (lax.broadcasted_iota(jnp.int32, (LANE, LANE), 0)
             == lax.broadcasted_iota(jnp.int32, (LANE, LANE), 1))

    def stack(x):
        return jnp.concatenate([jnp.where(low, x, 0.0), jnp.where(low, 0.0, x)], axis=0).astype(BF16)

    def head_sum(x):
        s_a = jnp.sum(jnp.where(low, x, 0.0), axis=-1, keepdims=True)
        s_b = jnp.sum(jnp.where(low, 0.0, x), axis=-1, keepdims=True)
        return jnp.where(low, s_a, s_b)

    npair = N_HEADS // 2
    items = [(q, p) for q in range(nch) for p in range(npair)]
    idx = range(len(items))
    blk = lambda arr, q, p: arr[q * c:(q + 1) * c, p * LANE:(p + 1) * LANE]
    cum = [blk(cum_all, q, p) for q, p in items]
    lw = [blk(logw, q, p) for q, p in items]
    r = [blk(r_all, q, p) for q, p in items]
    k2 = [blk(k2_all, q, p) for q, p in items]
    v = [blk(v_all, q, p) for q, p in items]
    kk = [blk(kk_all, q, p) for q, p in items]
    kk = [x * lax.rsqrt(jnp.maximum(head_sum(x * x), 1e-12)) for x in kk]
    gam = [jnp.exp(x) for x in cum]
    inv_gam = [jnp.exp(-x) for x in cum]
    gam_c = [jnp.exp(x[c - 1:c, :]) for x in cum]
    at = [-kk[i] * jnp.exp(cum[i] - lw[i]) for i in idx]
    bt = [kk[i] * blk(a_all, *items[i]) * inv_gam[i] for i in idx]
    kt = [k2[i] * inv_gam[i] for i in idx]
    rt = [r[i] * gam[i] for i in idx]
    s_at = [stack(x) for x in at]
    s_rt = [stack(x) for x in rt]
    s_v = [stack(x) for x in v]
    pm = [lax.dot_general(jnp.concatenate([s_at[i], s_rt[i]], axis=0),
                          jnp.concatenate([stack(bt[i]), stack(kt[i])], axis=0),
                          (((1,), (1,)), ((), ())), preferred_element_type=F32) for i in idx]
    gate_pieces(1)
    a_ab = [jnp.where(strict, x[0:c2, 0:c2], 0.0) for x in pm]
    a_ak = [jnp.where(strict, x[0:c2, c2:2 * c2], 0.0).astype(BF16) for x in pm]
    a_br = [jnp.where(incl, x[c2:2 * c2, 0:c2], 0.0).astype(BF16) for x in pm]
    a_kr = [jnp.where(incl, x[c2:2 * c2, c2:2 * c2], 0.0).astype(BF16) for x in pm]

    xinv = [jnp.where(ri == cj, 1.0, x) for x in a_ab]
    pw = [_dot(x, x) for x in a_ab]
    gate_pieces(1)
    steps = int(math.log2(c)) - 1
    for n in range(1, steps + 1):
        pw_bf = [x.astype(BF16) for x in pw]
        if n < steps:
            both = [jnp.dot(jnp.concatenate([pw_bf[i], xinv[i].astype(BF16)], axis=0), pw_bf[i],
                            preferred_element_type=F32) for i in idx]
            pw = [x[0:c2, :] for x in both]
            xinv = [xinv[i] + both[i][c2:2 * c2, :] for i in idx]
            gate_pieces(1)
        else:
            xinv = [xinv[i] + jnp.dot(xinv[i].astype(BF16), pw_bf[i], preferred_element_type=F32) for i in idx]

    gate_pieces(1)
    akv = [jnp.dot(a_ak[i], s_v[i], preferred_element_type=F32) for i in idx]
    gate_pieces(1)
    sol = [_dot(xinv[i], jnp.concatenate([s_at[i], akv[i].astype(BF16)], axis=1)) for i in idx]
    gate_pieces(1)
    zero_slab = jnp.zeros((c2, LANE), BF16)
    rhs2 = [jnp.concatenate([sol[i].astype(BF16), jnp.concatenate([zero_slab, s_v[i]], axis=1)], axis=0)
            for i in idx]
    asol = [jnp.dot(jnp.concatenate([a_br[i], a_kr[i]], axis=1), rhs2[i], preferred_element_type=F32)
            for i in idx]
    gate_pieces(1)
    mn = [_dot_tn(jnp.concatenate([stack(bt[i] * gam_c[i]), stack(kt[i] * gam_c[i])], axis=0), rhs2[i])
          for i in idx]
    gate_pieces(1)
    rh_m = [jnp.concatenate([(s_rt[i] + asol[i][:, 0:LANE]).astype(BF16),
                             (jnp.where(eye_p, gam_c[i], 0.0) + mn[i][:, 0:LANE]).astype(BF16)], axis=0)
            for i in idx]
    n_mat = [mn[i][:, LANE:2 * LANE] for i in idx]
    y_add = [asol[i][:, LANE:2 * LANE] for i in idx]

    st = [st_ref[p] for p in range(npair)]
    y_st = []
    for q in range(nch):
        both = [jnp.dot(rh_m[q * npair + p], st[p].astype(BF16), preferred_element_type=F32)
                for p in range(npair)]
        y_st += [both[p][0:c2, :] + y_add[q * npair + p] for p in range(npair)]
        st = [both[p][c2:c2 + LANE, :] + n_mat[q * npair + p] for p in range(npair)]
    for p in range(npair):
        st_ref[p] = st[p]
    gate_pieces(wg_ref.shape[1] // GATE_PIECE)

    outs = []
    for i in idx:
        q, p = items[i]
        ps = slice(p * LANE, (p + 1) * LANE)
        y = y_st[i][0:c, :] + y_st[i][c:c2, :]
        yc = y - head_sum(y) * (1.0 / HEAD)
        var = head_sum(yc * yc) * (1.0 / HEAD)
        yn = yc * lax.rsqrt(var + GN_EPS) * lng_ref[:, ps] + lnb_ref[:, ps]
        bonus = head_sum(r[i] * k2[i] * rk_ref[:, ps]) * v[i]
        outs.append((yn + bonus) * blk(g_all, q, p))
    y_ref[...] = jnp.concatenate(
        [jnp.concatenate(outs[q * npair:(q + 1) * npair], axis=1) for q in range(nch)], axis=0).astype(y_ref.dtype)

    @pl.when(ci == nc - 1)
    def _():
        for h in range(N_HEADS):
            o = (h % 2) * HEAD
            wkv_ref[0, h] = st_ref[h // 2, o:o + HEAD, o:o + HEAD].T


def rwkv_mix(pa, shift_prev, wkv0, x, prm, bsz, t, c, nch):
    rows = c * nch
    nc = t // rows
    vec = lambda w: pl.BlockSpec((1, w), lambda b, i: (0, 0))
    mat = lambda r, w: pl.BlockSpec((r, w), lambda b, i: (0, 0))
    return pl.pallas_call(
        functools.partial(_rwkv_body, c=c, nch=nch),
        grid=(bsz, nc),
        in_specs=[pl.BlockSpec((rows, PA_W), lambda b, i: (b * nc + i, 0)),
                  pl.BlockSpec((1, 8, PA_W), lambda b, i: (b, 0, 0)),
                  pl.BlockSpec((1, N_HEADS, HEAD, HEAD), lambda b, i: (b, 0, 0, 0)),
                  vec(PA_W), vec(RW), mat(LORA_PAD, RW), vec(RW), mat(LORA_PAD, RW), mat(GATE_LORA, RW),
                  vec(RW), vec(RW), vec(RW), vec(RW), vec(RW),
                  pl.BlockSpec((rows, D_MODEL), lambda b, i: (b * nc + i, 0)),
                  _resident(prm["g_mix"]), _resident(prm["w_g"])],
        out_specs=[pl.BlockSpec((rows, RW), lambda b, i: (b * nc + i, 0)),
                   pl.BlockSpec((1, N_HEADS, HEAD, HEAD), lambda b, i: (b, 0, 0, 0)),
                   pl.BlockSpec((rows, 2 * D_MODEL), lambda b, i: (b * nc + i, 0))],
        out_shape=[jax.ShapeDtypeStruct((bsz * t, RW), BF16),
                   jax.ShapeDtypeStruct((bsz, N_HEADS, HEAD, HEAD), F32),
                   jax.ShapeDtypeStruct((bsz * t, 2 * D_MODEL), F32)],
        scratch_shapes=[pltpu.VMEM((N_HEADS // 2, LANE, LANE), F32), pltpu.VMEM((8, PA_W), F32)],
        compiler_params=_cparams(("parallel", "arbitrary"), 61 * 1024 * 1024),
        name="rwkv_mix",
    )(pa, shift_prev, wkv0, prm["mu_a"], prm["w0"], prm["w2"], prm["a0"], prm["a2"], prm["g2"],
      prm["k_k"], prm["k_a"], prm["r_k"], prm["lnx_g"], prm["lnx_b"], x, prm["g_mix"], prm["w_g"])


def _rope_swap(slab):
    lane = lax.broadcasted_iota(jnp.int32, slab.shape, 1)
    return jnp.where(lane % QK_ROPE < QK_ROPE // 2, pltpu.roll(slab, LANE - QK_ROPE // 2, 1),
                     pltpu.roll(slab, QK_ROPE // 2, 1))


def _expand_store(lat_bf, kslab_bf, wukv_ref, kcat_ref, v_ref):
    kv = jnp.dot(lat_bf, wukv_ref[...], preferred_element_type=F32)
    for h in range(MLA_HEADS):
        kcat_ref[:, h * QK_CAT:h * QK_CAT + LANE] = kv[:, h * QK_NOPE:(h + 1) * QK_NOPE].astype(BF16)
        kcat_ref[:, h * QK_CAT + LANE:(h + 1) * QK_CAT] = kslab_bf
    v_ref[...] = kv[:, MLA_HEADS * QK_NOPE:].astype(BF16)


def _mla_prep_body(x_ref, g_ref, wb_ref, cs_ref, sn_ref, gkv_ref, wukv_ref,
                   qcat_ref, lat_ref, kr_ref, kcat_ref, v_ref):
    xn = _rms(x_ref[...], g_ref[...]).astype(BF16)
    pb = jnp.dot(xn, wb_ref[...], preferred_element_type=F32)
    cs = cs_ref[...]
    sn = sn_ref[...]
    lane = lax.broadcasted_iota(jnp.int32, cs.shape, 1)
    low = lane < QK_ROPE
    nq = MLA_HEADS * QK_NOPE
    for p in range(MLA_HEADS // 2):
        slab = pb[:, nq + p * LANE:nq + (p + 1) * LANE]
        roped = (slab * cs + _rope_swap(slab) * sn) * Q_SCALE
        for j in range(2):
            h = 2 * p + j
            qn = pb[:, h * QK_NOPE:(h + 1) * QK_NOPE] * Q_SCALE
            qcat_ref[:, h * QK_CAT:h * QK_CAT + LANE] = qn.astype(BF16)
            half = roped if j == 0 else pltpu.roll(roped, QK_ROPE, 1)
            qcat_ref[:, h * QK_CAT + LANE:(h + 1) * QK_CAT] = jnp.where(low, half, 0.0).astype(BF16)
    lat = _rms(pb[:, nq + QR_W:nq + QR_W + KV_LORA], gkv_ref[...])
    lat_ref[...] = lat
    kslab = pb[:, nq + QR_W + KV_LORA:PB_W]
    kroped = jnp.where(low, kslab * cs + _rope_swap(kslab) * sn, 0.0)
    kr_ref[...] = kroped[:, 0:QK_ROPE]
    _expand_store(lat.astype(BF16), kroped.astype(BF16), wukv_ref, kcat_ref, v_ref)


def mla_prep(x, g_mix, w_b, cs, sn, g_kv, w_ukv, tm):
    n = x.shape[0]
    ntab = cs.shape[0] // tm
    full = lambda a: pl.BlockSpec(a.shape, lambda i: (0, 0))
    return pl.pallas_call(
        _mla_prep_body,
        grid=(n // tm,),
        in_specs=[pl.BlockSpec((tm, D_MODEL), lambda i: (i, 0)), full(g_mix), full(w_b),
                  pl.BlockSpec((tm, LANE), lambda i: (i % ntab, 0)),
                  pl.BlockSpec((tm, LANE), lambda i: (i % ntab, 0)), full(g_kv), full(w_ukv)],
        out_specs=[pl.BlockSpec((tm, MLA_HEADS * QK_CAT), lambda i: (i, 0)),
                   pl.BlockSpec((tm, KV_LORA), lambda i: (i, 0)),
                   pl.BlockSpec((tm, QK_ROPE), lambda i: (i, 0)),
                   pl.BlockSpec((tm, MLA_HEADS * QK_CAT), lambda i: (i, 0)),
                   pl.BlockSpec((tm, MLA_HEADS * V_HEAD), lambda i: (i, 0))],
        out_shape=[jax.ShapeDtypeStruct((n, MLA_HEADS * QK_CAT), BF16),
                   jax.ShapeDtypeStruct((n, KV_LORA), F32),
                   jax.ShapeDtypeStruct((n, QK_ROPE), F32),
                   jax.ShapeDtypeStruct((n, MLA_HEADS * QK_CAT), BF16),
                   jax.ShapeDtypeStruct((n, MLA_HEADS * V_HEAD), BF16)],
        compiler_params=_cparams(("parallel",)),
        name="mla_prep",
    )(x, g_mix, w_b, cs, sn, g_kv, w_ukv)


def _softmax_block(s, v, m, l, acc):
    m_new = jnp.maximum(m, jnp.max(s, axis=-1, keepdims=True))
    alpha = jnp.exp2(m - m_new)
    p = jnp.exp2(s - m_new)
    l_new = alpha * l + jnp.sum(p, axis=-1, keepdims=True)
    acc_new = alpha * acc + jnp.dot(p.astype(BF16), v, preferred_element_type=F32)
    return m_new, l_new, acc_new


def _scores(q, kblk):
    return lax.dot_general(q, kblk, (((1,), (1,)), ((), ())), preferred_element_type=F32)


def _qk_cols(h):
    return slice(h * QK_CAT, (h + 1) * QK_CAT)


def _v_cols(h):
    return slice(h * V_HEAD, (h + 1) * V_HEAD)


def _attn_prompt_body(q_ref, k_ref, v_ref, o_ref, *, tq, hb):
    i = pl.program_id(2)
    start = pl.multiple_of(i * tq, tq)
    qc = lax.broadcasted_iota(jnp.int32, (tq, tq), 0) // CHUNK
    kc = lax.broadcasted_iota(jnp.int32, (tq, tq), 1) // CHUNK
    visible = kc <= qc
    init = []
    for h in range(hb):
        s = jnp.where(visible, _scores(q_ref[:, _qk_cols(h)], k_ref[pl.ds(start, tq), _qk_cols(h)]), NEG_INF)
        m = jnp.max(s, axis=-1, keepdims=True)
        p = jnp.exp2(s - m)
        init.append((m, jnp.sum(p, axis=-1, keepdims=True),
                     jnp.dot(p.astype(BF16), v_ref[pl.ds(start, tq), _v_cols(h)], preferred_element_type=F32)))

    def body(j, carry):
        off = pl.multiple_of(j * tq, tq)
        return tuple(_softmax_block(_scores(q_ref[:, _qk_cols(h)], k_ref[pl.ds(off, tq), _qk_cols(h)]),
                                    v_ref[pl.ds(off, tq), _v_cols(h)], *carry[h]) for h in range(hb))

    final = lax.fori_loop(0, i, body, tuple(init))
    for h in range(hb):
        m, l, acc = final[h]
        o_ref[:, _v_cols(h)] = (acc / l).astype(o_ref.dtype)


def attention_prompt(qcat, kcat, v, bsz, t, tq, hb):
    nq = t // tq
    return pl.pallas_call(
        functools.partial(_attn_prompt_body, tq=tq, hb=hb),
        grid=(bsz, MLA_HEADS // hb, nq),
        in_specs=[pl.BlockSpec((tq, hb * QK_CAT), lambda b, h, i: (b * nq + i, h)),
                  pl.BlockSpec((t, hb * QK_CAT), lambda b, h, i: (b, h)),
                  pl.BlockSpec((t, hb * V_HEAD), lambda b, h, i: (b, h))],
        out_specs=pl.BlockSpec((tq, hb * V_HEAD), lambda b, h, i: (b * nq + i, h)),
        out_shape=jax.ShapeDtypeStruct((bsz * t, MLA_HEADS * V_HEAD), BF16),
        compiler_params=_cparams(("parallel", "parallel", "arbitrary")),
        name="attention_prompt",
    )(qcat, kcat, v)


def _attn_sample_body(q_ref, lat_ref, kr_ref, pad_ref, wukv_ref, kn_ref, vn_ref, o_ref):
    kv = jnp.dot(lat_ref[...].astype(BF16), wukv_ref[...], preferred_element_type=F32)
    kslab = jnp.dot(kr_ref[...].astype(BF16), pad_ref[...], preferred_element_type=F32).astype(BF16)
    for h in range(MLA_HEADS):
        q = q_ref[:, _qk_cols(h)]
        k_nope = kv[:, h * QK_NOPE:(h + 1) * QK_NOPE].astype(BF16)
        s = _scores(q[:, 0:LANE], k_nope) + _scores(q[:, LANE:QK_CAT], kslab)
        m = jnp.max(s, axis=-1, keepdims=True)
        p = jnp.exp2(s - m)
        l = jnp.sum(p, axis=-1, keepdims=True)
        v_past = kv[:, MLA_HEADS * QK_NOPE + h * V_HEAD:MLA_HEADS * QK_NOPE + (h + 1) * V_HEAD].astype(BF16)
        acc = jnp.dot(p.astype(BF16), v_past, preferred_element_type=F32)
        m, l, acc = _softmax_block(_scores(q, kn_ref[:, _qk_cols(h)]), vn_ref[:, _v_cols(h)], m, l, acc)
        o_ref[:, _v_cols(h)] = (acc / l).astype(o_ref.dtype)


def attention_sample(qcat, lat_past, kr_past, w_ukv, kcat_new, v_new, bsz, t, past):
    qk_w, v_w = MLA_HEADS * QK_CAT, MLA_HEADS * V_HEAD
    pad = jnp.eye(QK_ROPE, LANE, dtype=BF16)
    return pl.pallas_call(
        _attn_sample_body,
        grid=(bsz,),
        in_specs=[pl.BlockSpec((t, qk_w), lambda b: (b, 0)),
                  pl.BlockSpec((past, KV_LORA), lambda b: (b, 0)),
                  pl.BlockSpec((past, QK_ROPE), lambda b: (b, 0)),
                  _resident(pad), _resident(w_ukv),
                  pl.BlockSpec((t, qk_w), lambda b: (b, 0)),
                  pl.BlockSpec((t, v_w), lambda b: (b, 0))],
        out_specs=pl.BlockSpec((t, v_w), lambda b: (b, 0)),
        out_shape=jax.ShapeDtypeStruct((bsz * t, v_w), BF16),
        compiler_params=_cparams(("parallel",)),
        name="attention_sample",
    )(qcat, lat_past, kr_past, pad, w_ukv, kcat_new, v_new)


def _combine_body(ya_ref, ob_ref, pg_ref, x_ref, woa_ref, wob_ref, wout_ref, gffn_ref, h_ref, hn_ref):
    y_a = jnp.dot(ya_ref[...], woa_ref[...], preferred_element_type=F32)
    y_b = jnp.dot(ob_ref[...], wob_ref[...], preferred_element_type=F32)
    gate_a = jax.nn.sigmoid(pg_ref[:, 0:D_MODEL])
    gate_b = jax.nn.sigmoid(pg_ref[:, D_MODEL:2 * D_MODEL])
    mix = (gate_a * y_a + gate_b * y_b).astype(BF16)
    h = x_ref[...] + jnp.dot(mix, wout_ref[...], preferred_element_type=F32)
    h_ref[...] = h
    hn_ref[...] = _rms(h, gffn_ref[...]).astype(BF16)


def combine(ya, ob, pg, x, w_o_a, w_o_b, w_out, g_ffn, tm):
    n = x.shape[0]
    row = lambda w: pl.BlockSpec((tm, w), lambda i: (i, 0))
    res = lambda a: pl.BlockSpec(a.shape, lambda i: (0, 0), pipeline_mode=pl.Buffered(1))
    return pl.pallas_call(
        _combine_body,
        grid=(n // tm,),
        in_specs=[row(RW), row(MLA_HEADS * V_HEAD), row(2 * D_MODEL), row(D_MODEL),
                  res(w_o_a), res(w_o_b), res(w_out), res(g_ffn)],
        out_specs=[row(D_MODEL), row(D_MODEL)],
        out_shape=[jax.ShapeDtypeStruct((n, D_MODEL), F32), jax.ShapeDtypeStruct((n, D_MODEL), BF16)],
        compiler_params=_cparams(("parallel",)),
        name="combine",
    )(ya, ob, pg, x, w_o_a, w_o_b, w_out, g_ffn)


HALO = 16


def _ffn_in_body(hn_ref, halo_ref, wg_ref, wu_ref, cwb_ref, st_ref, act_ref, cnew_ref, *,
                 tm, ts, tiles_per_seq, parts):
    i = pl.program_id(1)
    nseq = tm // ts
    tn = act_ref.shape[1]
    hn = hn_ref[...]
    t_in = lax.broadcasted_iota(jnp.int32, (tm, 1), 0) % ts
    first = (i % tiles_per_seq) == 0
    w = tn // parts
    for p in range(parts):
        cs = slice(p * w, (p + 1) * w)
        gate = jnp.dot(hn, wg_ref[:, cs], preferred_element_type=F32)
        up = jnp.dot(hn, wu_ref[:, cs], preferred_element_type=F32)
        if nseq == 1:
            prev2, prev1 = st_ref[0, 0:1, cs], st_ref[0, 1:2, cs]
        else:
            prev2 = jnp.broadcast_to(st_ref[:, 0:1, cs], (nseq, ts, w)).reshape(tm, w)
            prev1 = jnp.broadcast_to(st_ref[:, 1:2, cs], (nseq, ts, w)).reshape(tm, w)
        if tiles_per_seq > 1:
            ghalo = jnp.dot(halo_ref[...], wg_ref[:, cs], preferred_element_type=F32)
            prev2 = jnp.where(first, prev2, ghalo[HALO - 2:HALO - 1, :])
            prev1 = jnp.where(first, prev1, ghalo[HALO - 1:HALO, :])
        g1 = jnp.where(t_in == 0, prev1, pltpu.roll(gate, 1, 0))
        g2 = jnp.where(t_in == 0, prev2, jnp.where(t_in == 1, prev1, pltpu.roll(gate, 2, 0)))
        gate_c = g2 * cwb_ref[0:1, cs] + g1 * cwb_ref[1:2, cs] + gate * cwb_ref[2:3, cs] + cwb_ref[3:4, cs]
        act_ref[:, cs] = (jax.nn.silu(gate_c) * up).astype(act_ref.dtype)

        @pl.when((i % tiles_per_seq) == tiles_per_seq - 1)
        def _():
            cnew_ref[:, :, cs] = gate.reshape(nseq, ts, w)[:, ts - (CONV_W - 1):ts, :]


def ffn_in(hn, w_ffn_in, cwb, conv_state, bsz, t, tm, tn):
    n = bsz * t
    ts = min(t, tm)
    tps = t // ts
    nseq = tm // ts
    ncol = D_FF // tn
    r = tm // HALO
    return pl.pallas_call(
        functools.partial(_ffn_in_body, tm=tm, ts=ts, tiles_per_seq=tps, parts=1),
        grid=(ncol, n // tm),
        in_specs=[pl.BlockSpec((tm, D_MODEL), lambda j, i: (i, 0)),
                  pl.BlockSpec((HALO, D_MODEL), lambda j, i: (jnp.maximum(i * r - 1, 0), 0)),
                  pl.BlockSpec((D_MODEL, tn), lambda j, i: (0, j)),
                  pl.BlockSpec((D_MODEL, tn), lambda j, i: (0, j + ncol)),
                  pl.BlockSpec((8, tn), lambda j, i: (0, j)),
                  pl.BlockSpec((nseq, 8, tn), lambda j, i: (i // tps, 0, j))],
        out_specs=[pl.BlockSpec((tm, tn), lambda j, i: (i, j)),
                   pl.BlockSpec((nseq, CONV_W - 1, tn), lambda j, i: (i // tps, 0, j))],
        out_shape=[jax.ShapeDtypeStruct((n, D_FF), BF16),
                   jax.ShapeDtypeStruct((bsz, CONV_W - 1, D_FF), F32)],
        compiler_params=_cparams(("parallel", "arbitrary")),
        name="ffn_in",
    )(hn, hn, w_ffn_in, w_ffn_in, cwb, conv_state)


def _ffn_down_body(act_ref, w_ref, h_ref, g_ref, y_ref):
    y_ref[...] = _rms(h_ref[...] + jnp.dot(act_ref[...], w_ref[...], preferred_element_type=F32), g_ref[...])


def ffn_down(act, w_down, h, g_final, tm):
    n = h.shape[0]
    return pl.pallas_call(
        _ffn_down_body,
        grid=(n // tm,),
        in_specs=[pl.BlockSpec((tm, D_FF), lambda i: (i, 0)), _resident(w_down),
                  pl.BlockSpec((tm, D_MODEL), lambda i: (i, 0)), _resident(g_final)],
        out_specs=pl.BlockSpec((tm, D_MODEL), lambda i: (i, 0)),
        out_shape=jax.ShapeDtypeStruct((n, D_MODEL), F32),
        compiler_params=_cparams(("parallel",)),
        name="ffn_down",
    )(act, w_down, h, g_final)


def _pad_cols(w, width):
    return jnp.pad(w, ((0, 0), (0, width - w.shape[1])))


def _prepare_params(g_mix, w_in, mu_shift, w0, w2, a0, a2, g2, k_k, k_a, r_k, lnx_g, lnx_b, w_o_a, g_kv,
                    w_ukv, w_o_b, w_out, g_ffn, w_ffn_in, conv_w, conv_b, w_ffn_down, g_final):
    o_zw = 3 * RW
    o_za = o_zw + DECAY_LORA
    o_zg = o_za + AAA_LORA
    o_q = RWKV_PROJ
    o_ckv = o_q + MLA_Q_WIDTH
    o_kr = o_ckv + KV_LORA
    o_g = RWKV_PROJ + MLA_PROJ

    def sect_a(m):
        return jnp.concatenate([m[:, :o_zw], _pad_cols(m[:, o_zw:o_za], LORA_PAD),
                                _pad_cols(m[:, o_za:o_zg], LORA_PAD), m[:, o_zg:RWKV_PROJ]], axis=1)

    wq = w_in[:, o_q:o_ckv].reshape(D_MODEL, MLA_HEADS, QK_NOPE + QK_ROPE)
    w_b = jnp.concatenate([wq[:, :, :QK_NOPE].reshape(D_MODEL, -1), wq[:, :, QK_NOPE:].reshape(D_MODEL, -1),
                           w_in[:, o_ckv:o_kr], _pad_cols(w_in[:, o_kr:o_g], LANE)], axis=1)
    wkv = w_ukv.reshape(KV_LORA, MLA_HEADS, QK_NOPE + V_HEAD)
    w_ukv_p = jnp.concatenate([wkv[:, :, :QK_NOPE].reshape(KV_LORA, -1),
                               wkv[:, :, QK_NOPE:].reshape(KV_LORA, -1)], axis=1)
    pad_rows = lambda m: jnp.pad(m, ((0, LORA_PAD - m.shape[0]), (0, 0)))
    row = lambda vct: vct.reshape(1, -1)
    cwb = jnp.concatenate([conv_w, conv_b.reshape(1, D_FF), jnp.zeros((8 - CONV_W - 1, D_FF), F32)], axis=0)
    return dict(
        g_mix=row(g_mix), w_a=sect_a(w_in[:, :RWKV_PROJ]).astype(BF16), w_b=w_b.astype(BF16),
        w_g=w_in[:, o_g:].astype(BF16), mu_a=sect_a(row(mu_shift)),
        w0=row(w0), w2=pad_rows(w2).astype(BF16), a0=row(a0), a2=pad_rows(a2).astype(BF16),
        g2=g2.astype(BF16), k_k=row(k_k), k_a=row(k_a), r_k=row(r_k.reshape(-1)),
        lnx_g=row(lnx_g), lnx_b=row(lnx_b), w_o_a=w_o_a.astype(BF16), g_kv=row(g_kv),
        w_ukv=w_ukv_p.astype(BF16), w_o_b=w_o_b.astype(BF16), w_out=w_out.astype(BF16),
        g_ffn=row(g_ffn), w_ffn_in=w_ffn_in.astype(BF16), cwb=cwb, w_ffn_down=w_ffn_down.astype(BF16),
        g_final=row(g_final), sect_a=sect_a)


def _rope_tables(pos):
    half = QK_ROPE // 2
    inv = ROPE_THETA ** (-jnp.arange(half, dtype=F32) / half)
    ang = pos.astype(F32)[:, None] * inv[None, :]
    cos, sin = jnp.cos(ang), jnp.sin(ang)
    return jnp.tile(cos, (1, 4)), jnp.tile(jnp.concatenate([-sin, sin], axis=1), (1, 2))


def _tile_rows(n, pref):
    return pref if n % pref == 0 else n


def _layer(x, pos, shift_prev, wkv0, conv_prev, past, prm):
    bsz, t, _ = x.shape
    n = bsz * t
    x2 = x.reshape(n, D_MODEL)
    tm = _tile_rows(n, 256)

    pa = norm_matmul(x2, prm["g_mix"], prm["w_a"], tm)

    shift_p = jnp.pad(prm["sect_a"](shift_prev.reshape(bsz, RWKV_PROJ)).reshape(bsz, 1, PA_W),
                      ((0, 0), (0, 7), (0, 0)))
    ya, wkv_new, pg = rwkv_mix(pa, shift_p, wkv0, x2, prm, bsz, t, min(t, 64), 4 if t >= 256 else 1)
    last = pa.reshape(bsz, t, PA_W)[:, t - 1]
    o_zw = 3 * RW
    shift_new = jnp.concatenate([last[:, :o_zw], last[:, o_zw:o_zw + DECAY_LORA],
                                 last[:, o_zw + LORA_PAD:o_zw + LORA_PAD + AAA_LORA],
                                 last[:, o_zw + 2 * LORA_PAD:]], axis=1).reshape(bsz, 1, RWKV_PROJ)

    cs, sn = _rope_tables(pos)
    if past is None:
        tmb = _tile_rows(t, 256)
    else:
        tmb = n
        cs, sn = jnp.tile(cs, (bsz, 1)), jnp.tile(sn, (bsz, 1))
    qcat, lat, kr, kcat, v = mla_prep(x2, prm["g_mix"], prm["w_b"], cs, sn, prm["g_kv"], prm["w_ukv"], tmb)
    if past is None:
        ob = attention_prompt(qcat, kcat, v, bsz, t, 512, 4)
    else:
        plat, pkr = past
        plen = plat.shape[1]
        ob = attention_sample(qcat, plat.reshape(bsz * plen, KV_LORA), pkr.reshape(bsz * plen, QK_ROPE),
                              prm["w_ukv"], kcat, v, bsz, t, plen)

    h, hn = combine(ya, ob, pg, x2, prm["w_o_a"], prm["w_o_b"], prm["w_out"], prm["g_ffn"], tm)
    conv_p = jnp.pad(conv_prev, ((0, 0), (0, 8 - (CONV_W - 1)), (0, 0)))
    tm_ffn = _tile_rows(n, 1024) if t >= 1024 else _tile_rows(n, 512)
    act, conv_new = ffn_in(hn, prm["w_ffn_in"], prm["cwb"], conv_p, bsz, t, tm_ffn, 512)
    y = ffn_down(act, prm["w_ffn_down"], h, prm["g_final"], tm)
    return (y.reshape(bsz, t, D_MODEL), lat.reshape(bsz, t, KV_LORA), kr.reshape(bsz, t, QK_ROPE),
            wkv_new, shift_new, conv_new)


def kernel(x_prompt, x_sample, cache_mla_latent, cache_mla_krope, state_rwkv_wkv, state_rwkv_shift, state_ffn_conv, g_mix, w_in, mu_shift, w0, w2, a0, a2, g2, k_k, k_a, r_k, lnx_g, lnx_b, w_o_a, g_kv, w_ukv, w_o_b, w_out, g_ffn, w_ffn_in, conv_w, conv_b, w_ffn_down, g_final):
    depth = w_in.shape[0]
    assert depth == 1, "single-layer step"
    bp, tp, _ = x_prompt.shape
    ts = x_sample.shape[1]
    past = cache_mla_latent.shape[2]
    prm = _prepare_params(g_mix[0], w_in[0], mu_shift[0], w0[0], w2[0], a0[0], a2[0], g2[0], k_k[0], k_a[0],
                          r_k[0], lnx_g[0], lnx_b[0], w_o_a[0], g_kv[0], w_ukv[0], w_o_b[0], w_out[0],
                          g_ffn[0], w_ffn_in[0], conv_w[0], conv_b[0], w_ffn_down[0], g_final)
    dt = x_prompt.dtype
    out_p = _layer(x_prompt, jnp.arange(tp), jnp.zeros((bp, 1, RWKV_PROJ), dt),
                   jnp.zeros((bp, N_HEADS, HEAD, HEAD), F32), jnp.zeros((bp, CONV_W - 1, D_FF), dt),
                   None, prm)
    out_s = _layer(x_sample, past + jnp.arange(ts), state_rwkv_shift[0], state_rwkv_wkv[0],
                   state_ffn_conv[0], (cache_mla_latent[0], cache_mla_krope[0]), prm)
    lead = lambda a: a[None].astype(dt)
    return (out_p[0], out_s[0],
            lead(out_p[1]), lead(out_p[2]), lead(out_p[3]), lead(out_p[4]), lead(out_p[5]),
            lead(out_s[1]), lead(out_s[2]), lead(out_s[3]), lead(out_s[4]), lead(out_s[5]))
```

```python
import functools
import math

import jax
import jax.numpy as jnp
from jax import lax
from jax.experimental import pallas as pl
from jax.experimental.pallas import tpu as pltpu

F32 = jnp.float32
BF16 = jnp.bfloat16

D_MODEL = 2048
NORM_EPS = 1e-6
HEAD = 64
N_HEADS = 16
RW = N_HEADS * HEAD
DECAY_LORA = 96
AAA_LORA = 96
GATE_LORA = 256
GN_EPS = 64e-5
RWKV_PROJ = 3 * RW + DECAY_LORA + AAA_LORA + GATE_LORA
MLA_HEADS = 8
QK_NOPE = 128
QK_ROPE = 64
V_HEAD = 128
KV_LORA = 512
ROPE_THETA = 10000.0
MLA_Q_WIDTH = MLA_HEADS * (QK_NOPE + QK_ROPE)
MLA_PROJ = MLA_Q_WIDTH + KV_LORA + QK_ROPE
SOFTMAX_SCALE = (QK_NOPE + QK_ROPE) ** -0.5
Q_SCALE = SOFTMAX_SCALE * math.log2(math.e)
NEG_INF = -1e30
CHUNK = 64
D_FF = 5632
CONV_W = 3

LANE = 128
LORA_PAD = 128
GATE_PIECE = 256
PA_W = 3 * RW + 2 * LORA_PAD + GATE_LORA
QR_W = MLA_HEADS * QK_ROPE
PB_W = MLA_HEADS * QK_NOPE + QR_W + KV_LORA + LANE
QK_CAT = 2 * LANE
VMEM_LIMIT = 56 * 1024 * 1024
RWKV_VMEM_LIMIT = 61 * 1024 * 1024


def _cparams(sem, vmem=VMEM_LIMIT):
    return pltpu.CompilerParams(dimension_semantics=sem, vmem_limit_bytes=vmem)


def _dot(a, b):
    return jnp.dot(a.astype(BF16), b.astype(BF16), preferred_element_type=F32)


def _dot_nt(a, b):
    return lax.dot_general(a.astype(BF16), b.astype(BF16), (((1,), (1,)), ((), ())),
                           preferred_element_type=F32)


def _dot_tn(a, b):
    return lax.dot_general(a.astype(BF16), b.astype(BF16), (((0,), (0,)), ((), ())),
                           preferred_element_type=F32)


def _rms(x, g):
    return x * lax.rsqrt(jnp.mean(x * x, axis=-1, keepdims=True) + NORM_EPS) * g


def _resident(a):
    nd = a.ndim
    return pl.BlockSpec(a.shape, lambda *_: (0,) * nd, pipeline_mode=pl.Buffered(1))


def _norm_matmul_body(x_ref, g_ref, w_ref, o_ref, *, parts):
    rows = x_ref.shape[0] // parts
    for p in range(parts):
        rs = slice(p * rows, (p + 1) * rows)
        xn = _rms(x_ref[rs, :], g_ref[...]).astype(BF16)
        o_ref[rs, :] = jnp.dot(xn, w_ref[...], preferred_element_type=F32).astype(o_ref.dtype)


def norm_matmul(x, g, w, tm, out_dtype=F32):
    n, k = x.shape
    m = w.shape[1]
    return pl.pallas_call(
        functools.partial(_norm_matmul_body, parts=2),
        grid=(n // tm,),
        in_specs=[pl.BlockSpec((tm, k), lambda i: (i, 0)), _resident(g), _resident(w)],
        out_specs=pl.BlockSpec((tm, m), lambda i: (i, 0)),
        out_shape=jax.ShapeDtypeStruct((n, m), out_dtype),
        compiler_params=_cparams(("parallel",)),
        name="norm_matmul",
    )(x, g, w)


def _rwkv_body(pa_ref, shift_ref, wkv0_ref, mu_ref, w0_ref, w2_ref, a0_ref, a2_ref, g2_ref,
               kk_ref, ka_ref, rk_ref, lng_ref, lnb_ref, *rest, c, nch, fuse_gate):
    if fuse_gate:
        x_ref, gmix_ref, wg_ref, y_ref, wkv_ref, pg_ref, st_ref, prev_ref = rest
    else:
        y_ref, wkv_ref, st_ref, prev_ref = rest
    ci = pl.program_id(1)
    nc = pl.num_programs(1)
    rows = nch * c

    @pl.when(ci == 0)
    def _():
        prev_ref[...] = shift_ref[0]
        st_ref[...] = jnp.zeros(st_ref.shape, F32)
        for h in range(N_HEADS):
            o = (h % 2) * HEAD
            st_ref[h // 2, o:o + HEAD, o:o + HEAD] = wkv0_ref[0, h].T

    pa = pa_ref[...]
    row = lax.broadcasted_iota(jnp.int32, (rows, 1), 0)
    p_prev = jnp.where(row == 0, prev_ref[0:1, :], pltpu.roll(pa, 1, 0))
    prev_ref[0:1, :] = pa[rows - 1:rows, :]
    z = pa + mu_ref[...] * (p_prev - pa)

    n_pieces = 2 * D_MODEL // GATE_PIECE if fuse_gate else 0
    pieces = iter(range(n_pieces))
    xn = _rms(x_ref[...], gmix_ref[...]).astype(BF16) if fuse_gate else None

    def gate_pieces(k):
        for _ in range(k):
            j = next(pieces, None)
            if j is not None:
                cs = slice(j * GATE_PIECE, (j + 1) * GATE_PIECE)
                pg_ref[:, cs] = jnp.dot(xn, wg_ref[:, cs], preferred_element_type=F32)

    gate_pieces(2)

    r_all = z[:, 0:RW]
    k_all = z[:, RW:2 * RW]
    v_all = z[:, 2 * RW:3 * RW]
    zw = z[:, 3 * RW:3 * RW + LORA_PAD]
    za = z[:, 3 * RW + LORA_PAD:3 * RW + 2 * LORA_PAD]
    zg = z[:, 3 * RW + 2 * LORA_PAD:PA_W]

    w_raw = w0_ref[...] + _dot(jnp.tanh(zw), w2_ref[...])
    logw = -jnp.exp(-jax.nn.softplus(-w_raw) - 0.5)
    a_all = jax.nn.sigmoid(a0_ref[...] + _dot(za, a2_ref[...]))
    g_all = _dot(jax.nn.sigmoid(zg), g2_ref[...])
    gate_pieces(1)
    kk_all = k_all * kk_ref[...]
    k2_all = k_all * (1.0 + (a_all - 1.0) * ka_ref[...])

    ti = lax.broadcasted_iota(jnp.int32, (rows, rows), 0)
    tj = lax.broadcasted_iota(jnp.int32, (rows, rows), 1)
    tri = ((ti >= tj) & (ti // c == tj // c)).astype(BF16)
    w_hi = logw.astype(BF16)
    w_mid = (logw - w_hi.astype(F32)).astype(BF16)
    w_lo = (logw - w_hi.astype(F32) - w_mid.astype(F32)).astype(BF16)
    cum_all = (jnp.dot(tri, w_hi, preferred_element_type=F32) + jnp.dot(tri, w_mid, preferred_element_type=F32)
               + jnp.dot(tri, w_lo, preferred_element_type=F32))

    c2 = 2 * c
    low = lax.broadcasted_iota(jnp.int32, (c, LANE), 1) < HEAD
    ri = lax.broadcasted_iota(jnp.int32, (c2, c2), 0)
    cj = lax.broadcasted_iota(jnp.int32, (c2, c2), 1)
    same = (ri // c) == (cj // c)
    strict = same & (ri > cj)
    incl = same & (ri >= cj)
    eye_p = (lax.broadcasted_iota(jnp.int32, (LANE, LANE), 0)
             == lax.broadcasted_iota(jnp.int32, (LANE, LANE), 1))

    def stack(x):
        return jnp.concatenate([jnp.where(low, x, 0.0), jnp.where(low, 0.0, x)], axis=0).astype(BF16)

    def head_sum(x):
        s_a = jnp.sum(jnp.where(low, x, 0.0), axis=-1, keepdims=True)
        s_b = jnp.sum(jnp.where(low, 0.0, x), axis=-1, keepdims=True)
        return jnp.where(low, s_a, s_b)

    npair = N_HEADS // 2
    items = [(q, p) for q in range(nch) for p in range(npair)]
    idx = range(len(items))
    blk = lambda arr, q, p: arr[q * c:(q + 1) * c, p * LANE:(p + 1) * LANE]
    cum = [blk(cum_all, q, p) for q, p in items]
    lw = [blk(logw, q, p) for q, p in items]
    r = [blk(r_all, q, p) for q, p in items]
    k2 = [blk(k2_all, q, p) for q, p in items]
    v = [blk(v_all, q, p) for q, p in items]
    kk = [blk(kk_all, q, p) for q, p in items]
    kk = [x * lax.rsqrt(jnp.maximum(head_sum(x * x), 1e-12)) for x in kk]
    gam = [jnp.exp(x) for x in cum]
    inv_gam = [jnp.exp(-x) for x in cum]
    gam_c = [jnp.exp(x[c - 1:c, :]) for x in cum]
    at = [-kk[i] * jnp.exp(cum[i] - lw[i]) for i in idx]
    bt = [kk[i] * blk(a_all, *items[i]) * inv_gam[i] for i in idx]
    kt = [k2[i] * inv_gam[i] for i in idx]
    rt = [r[i] * gam[i] for i in idx]
    s_at = [stack(x) for x in at]
    s_rt = [stack(x) for x in rt]
    s_v = [stack(x) for x in v]
    pm = [lax.dot_general(jnp.concatenate([s_at[i], s_rt[i]], axis=0),
                          jnp.concatenate([stack(bt[i]), stack(kt[i])], axis=0),
                          (((1,), (1,)), ((), ())), preferred_element_type=F32) for i in idx]
    gate_pieces(1)
    a_ab = [jnp.where(strict, x[0:c2, 0:c2], 0.0) for x in pm]
    a_ak = [jnp.where(strict, x[0:c2, c2:2 * c2], 0.0).astype(BF16) for x in pm]
    a_br = [jnp.where(incl, x[c2:2 * c2, 0:c2], 0.0).astype(BF16) for x in pm]
    a_kr = [jnp.where(incl, x[c2:2 * c2, c2:2 * c2], 0.0).astype(BF16) for x in pm]

    xinv = [jnp.where(ri == cj, 1.0, x) for x in a_ab]
    pw = [_dot(x, x) for x in a_ab]
    gate_pieces(1)
    steps = int(math.log2(c)) - 1
    for n in range(1, steps + 1):
        pw_bf = [x.astype(BF16) for x in pw]
        if n < steps:
            both = [jnp.dot(jnp.concatenate([pw_bf[i], xinv[i].astype(BF16)], axis=0), pw_bf[i],
                            preferred_element_type=F32) for i in idx]
            pw = [x[0:c2, :] for x in both]
            xinv = [xinv[i] + both[i][c2:2 * c2, :] for i in idx]
            gate_pieces(1)
        else:
            xinv = [xinv[i] + jnp.dot(xinv[i].astype(BF16), pw_bf[i], preferred_element_type=F32) for i in idx]

    gate_pieces(1)
    akv = [jnp.dot(a_ak[i], s_v[i], preferred_element_type=F32) for i in idx]
    gate_pieces(1)
    sol = [_dot(xinv[i], jnp.concatenate([s_at[i], akv[i].astype(BF16)], axis=1)) for i in idx]
    gate_pieces(1)
    zero_slab = jnp.zeros((c2, LANE), BF16)
    rhs2 = [jnp.concatenate([sol[i].astype(BF16), jnp.concatenate([zero_slab, s_v[i]], axis=1)], axis=0)
            for i in idx]
    asol = [jnp.dot(jnp.concatenate([a_br[i], a_kr[i]], axis=1), rhs2[i], preferred_element_type=F32)
            for i in idx]
    gate_pieces(1)
    mn = [_dot_tn(jnp.concatenate([stack(bt[i] * gam_c[i]), stack(kt[i] * gam_c[i])], axis=0), rhs2[i])
          for i in idx]
    gate_pieces(1)
    rh_m = [jnp.concatenate([(s_rt[i] + asol[i][:, 0:LANE]).astype(BF16),
                             (jnp.where(eye_p, gam_c[i], 0.0) + mn[i][:, 0:LANE]).astype(BF16)], axis=0)
            for i in idx]
    n_mat = [mn[i][:, LANE:2 * LANE] for i in idx]
    y_add = [asol[i][:, LANE:2 * LANE] for i in idx]

    st = [st_ref[p] for p in range(npair)]
    y_st = []
    for q in range(nch):
        both = [jnp.dot(rh_m[q * npair + p], st[p].astype(BF16), preferred_element_type=F32)
                for p in range(npair)]
        y_st += [both[p][0:c2, :] + y_add[q * npair + p] for p in range(npair)]
        st = [both[p][c2:c2 + LANE, :] + n_mat[q * npair + p] for p in range(npair)]
    for p in range(npair):
        st_ref[p] = st[p]
    gate_pieces(n_pieces)

    outs = []
    for i in idx:
        q, p = items[i]
        ps = slice(p * LANE, (p + 1) * LANE)
        y = y_st[i][0:c, :] + y_st[i][c:c2, :]
        yc = y - head_sum(y) * (1.0 / HEAD)
        var = head_sum(yc * yc) * (1.0 / HEAD)
        yn = yc * lax.rsqrt(var + GN_EPS) * lng_ref[:, ps] + lnb_ref[:, ps]
        bonus = head_sum(r[i] * k2[i] * rk_ref[:, ps]) * v[i]
        outs.append((yn + bonus) * blk(g_all, q, p))
    y_ref[...] = jnp.concatenate(
        [jnp.concatenate(outs[q * npair:(q + 1) * npair], axis=1) for q in range(nch)], axis=0).astype(y_ref.dtype)

    @pl.when(ci == nc - 1)
    def _():
        for h in range(N_HEADS):
            o = (h % 2) * HEAD
            wkv_ref[0, h] = st_ref[h // 2, o:o + HEAD, o:o + HEAD].T


def rwkv_mix(pa, shift_prev, wkv0, x, prm, bsz, t, c, nch):
    rows = c * nch
    nc = t // rows
    fuse_gate = x is not None
    vec = lambda w: pl.BlockSpec((1, w), lambda b, i: (0, 0))
    mat = lambda r, w: pl.BlockSpec((r, w), lambda b, i: (0, 0))
    in_specs = [pl.BlockSpec((rows, PA_W), lambda b, i: (b * nc + i, 0)),
                pl.BlockSpec((1, 8, PA_W), lambda b, i: (b, 0, 0)),
                pl.BlockSpec((1, N_HEADS, HEAD, HEAD), lambda b, i: (b, 0, 0, 0)),
                vec(PA_W), vec(RW), mat(LORA_PAD, RW), vec(RW), mat(LORA_PAD, RW), mat(GATE_LORA, RW),
                vec(RW), vec(RW), vec(RW), vec(RW), vec(RW)]
    out_specs = [pl.BlockSpec((rows, RW), lambda b, i: (b * nc + i, 0)),
                 pl.BlockSpec((1, N_HEADS, HEAD, HEAD), lambda b, i: (b, 0, 0, 0))]
    out_shape = [jax.ShapeDtypeStruct((bsz * t, RW), BF16),
                 jax.ShapeDtypeStruct((bsz, N_HEADS, HEAD, HEAD), F32)]
    args = [pa, shift_prev, wkv0, prm["mu_a"], prm["w0"], prm["w2"], prm["a0"], prm["a2"], prm["g2"],
            prm["k_k"], prm["k_a"], prm["r_k"], prm["lnx_g"], prm["lnx_b"]]
    if fuse_gate:
        in_specs += [pl.BlockSpec((rows, D_MODEL), lambda b, i: (b * nc + i, 0)),
                     _resident(prm["g_mix"]), _resident(prm["w_g"])]
        out_specs.append(pl.BlockSpec((rows, 2 * D_MODEL), lambda b, i: (b * nc + i, 0)))
        out_shape.append(jax.ShapeDtypeStruct((bsz * t, 2 * D_MODEL), F32))
        args += [x, prm["g_mix"], prm["w_g"]]
    return pl.pallas_call(
        functools.partial(_rwkv_body, c=c, nch=nch, fuse_gate=fuse_gate),
        grid=(bsz, nc),
        in_specs=in_specs,
        out_specs=out_specs,
        out_shape=out_shape,
        scratch_shapes=[pltpu.VMEM((N_HEADS // 2, LANE, LANE), F32), pltpu.VMEM((8, PA_W), F32)],
        compiler_params=_cparams(("parallel", "arbitrary"), RWKV_VMEM_LIMIT),
        name="rwkv_mix",
    )(*args)


def _rope_swap(slab):
    lane = lax.broadcasted_iota(jnp.int32, slab.shape, 1)
    return jnp.where(lane % QK_ROPE < QK_ROPE // 2, pltpu.roll(slab, LANE - QK_ROPE // 2, 1),
                     pltpu.roll(slab, QK_ROPE // 2, 1))


def _expand_store(lat_bf, kslab_bf, wukv_ref, kcat_ref, v_ref):
    kv = jnp.dot(lat_bf, wukv_ref[...], preferred_element_type=F32)
    for h in range(MLA_HEADS):
        kcat_ref[:, h * QK_CAT:h * QK_CAT + LANE] = kv[:, h * QK_NOPE:(h + 1) * QK_NOPE].astype(BF16)
        kcat_ref[:, h * QK_CAT + LANE:(h + 1) * QK_CAT] = kslab_bf
    v_ref[...] = kv[:, MLA_HEADS * QK_NOPE:].astype(BF16)


def _mla_prep_body(x_ref, g_ref, wb_ref, cs_ref, sn_ref, gkv_ref, wukv_ref,
                   qcat_ref, lat_ref, kr_ref, kcat_ref, v_ref):
    xn = _rms(x_ref[...], g_ref[...]).astype(BF16)
    pb = jnp.dot(xn, wb_ref[...], preferred_element_type=F32)
    cs = cs_ref[...]
    sn = sn_ref[...]
    lane = lax.broadcasted_iota(jnp.int32, cs.shape, 1)
    low = lane < QK_ROPE
    nq = MLA_HEADS * QK_NOPE
    for p in range(MLA_HEADS // 2):
        slab = pb[:, nq + p * LANE:nq + (p + 1) * LANE]
        roped = (slab * cs + _rope_swap(slab) * sn) * Q_SCALE
        for j in range(2):
            h = 2 * p + j
            qn = pb[:, h * QK_NOPE:(h + 1) * QK_NOPE] * Q_SCALE
            qcat_ref[:, h * QK_CAT:h * QK_CAT + LANE] = qn.astype(BF16)
            half = roped if j == 0 else pltpu.roll(roped, QK_ROPE, 1)
            qcat_ref[:, h * QK_CAT + LANE:(h + 1) * QK_CAT] = jnp.where(low, half, 0.0).astype(BF16)
    lat = _rms(pb[:, nq + QR_W:nq + QR_W + KV_LORA], gkv_ref[...])
    lat_ref[...] = lat
    kslab = pb[:, nq + QR_W + KV_LORA:PB_W]
    kroped = jnp.where(low, kslab * cs + _rope_swap(kslab) * sn, 0.0)
    kr_ref[...] = kroped[:, 0:QK_ROPE]
    _expand_store(lat.astype(BF16), kroped.astype(BF16), wukv_ref, kcat_ref, v_ref)


def mla_prep(x, g_mix, w_b, cs, sn, g_kv, w_ukv, tm):
    n = x.shape[0]
    ntab = cs.shape[0] // tm
    full = lambda a: pl.BlockSpec(a.shape, lambda i: (0, 0))
    return pl.pallas_call(
        _mla_prep_body,
        grid=(n // tm,),
        in_specs=[pl.BlockSpec((tm, D_MODEL), lambda i: (i, 0)), full(g_mix), full(w_b),
                  pl.BlockSpec((tm, LANE), lambda i: (i % ntab, 0)),
                  pl.BlockSpec((tm, LANE), lambda i: (i % ntab, 0)), full(g_kv), full(w_ukv)],
        out_specs=[pl.BlockSpec((tm, MLA_HEADS * QK_CAT), lambda i: (i, 0)),
                   pl.BlockSpec((tm, KV_LORA), lambda i: (i, 0)),
                   pl.BlockSpec((tm, QK_ROPE), lambda i: (i, 0)),
                   pl.BlockSpec((tm, MLA_HEADS * QK_CAT), lambda i: (i, 0)),
                   pl.BlockSpec((tm, MLA_HEADS * V_HEAD), lambda i: (i, 0))],
        out_shape=[jax.ShapeDtypeStruct((n, MLA_HEADS * QK_CAT), BF16),
                   jax.ShapeDtypeStruct((n, KV_LORA), F32),
                   jax.ShapeDtypeStruct((n, QK_ROPE), F32),
                   jax.ShapeDtypeStruct((n, MLA_HEADS * QK_CAT), BF16),
                   jax.ShapeDtypeStruct((n, MLA_HEADS * V_HEAD), BF16)],
        compiler_params=_cparams(("parallel",)),
        name="mla_prep",
    )(x, g_mix, w_b, cs, sn, g_kv, w_ukv)


def _softmax_block(s, v, m, l, acc):
    m_new = jnp.maximum(m, jnp.max(s, axis=-1, keepdims=True))
    alpha = jnp.exp2(m - m_new)
    p = jnp.exp2(s - m_new)
    l_new = alpha * l + jnp.sum(p, axis=-1, keepdims=True)
    acc_new = alpha * acc + jnp.dot(p.astype(BF16), v, preferred_element_type=F32)
    return m_new, l_new, acc_new


def _scores(q, kblk):
    return lax.dot_general(q, kblk, (((1,), (1,)), ((), ())), preferred_element_type=F32)


def _qk_cols(h):
    return slice(h * QK_CAT, (h + 1) * QK_CAT)


def _v_cols(h):
    return slice(h * V_HEAD, (h + 1) * V_HEAD)


def _attn_prompt_body(q_ref, k_ref, v_ref, o_ref, *, tq, hb):
    i = pl.program_id(2)
    start = pl.multiple_of(i * tq, tq)
    qc = lax.broadcasted_iota(jnp.int32, (tq, tq), 0) // CHUNK
    kc = lax.broadcasted_iota(jnp.int32, (tq, tq), 1) // CHUNK
    visible = kc <= qc
    init = []
    for h in range(hb):
        s = jnp.where(visible, _scores(q_ref[:, _qk_cols(h)], k_ref[pl.ds(start, tq), _qk_cols(h)]), NEG_INF)
        m = jnp.max(s, axis=-1, keepdims=True)
        p = jnp.exp2(s - m)
        init.append((m, jnp.sum(p, axis=-1, keepdims=True),
                     jnp.dot(p.astype(BF16), v_ref[pl.ds(start, tq), _v_cols(h)], preferred_element_type=F32)))

    def body(j, carry):
        off = pl.multiple_of(j * tq, tq)
        return tuple(_softmax_block(_scores(q_ref[:, _qk_cols(h)], k_ref[pl.ds(off, tq), _qk_cols(h)]),
                                    v_ref[pl.ds(off, tq), _v_cols(h)], *carry[h]) for h in range(hb))

    final = lax.fori_loop(0, i, body, tuple(init))
    for h in range(hb):
        m, l, acc = final[h]
        o_ref[:, _v_cols(h)] = (acc / l).astype(o_ref.dtype)


def attention_prompt(qcat, kcat, v, bsz, t, tq, hb):
    nq = t // tq
    return pl.pallas_call(
        functools.partial(_attn_prompt_body, tq=tq, hb=hb),
        grid=(bsz, MLA_HEADS // hb, nq),
        in_specs=[pl.BlockSpec((tq, hb * QK_CAT), lambda b, h, i: (b * nq + i, h)),
                  pl.BlockSpec((t, hb * QK_CAT), lambda b, h, i: (b, h)),
                  pl.BlockSpec((t, hb * V_HEAD), lambda b, h, i: (b, h))],
        out_specs=pl.BlockSpec((tq, hb * V_HEAD), lambda b, h, i: (b * nq + i, h)),
        out_shape=jax.ShapeDtypeStruct((bsz * t, MLA_HEADS * V_HEAD), BF16),
        compiler_params=_cparams(("parallel", "parallel", "arbitrary")),
        name="attention_prompt",
    )(qcat, kcat, v)


def _attn_sample_body(q_ref, latp_ref, krp_ref, latn_ref, krn_ref, pad_ref, wukv_ref, o_ref):
    t = q_ref.shape[0]
    nq = MLA_HEADS * QK_NOPE
    q_lat = jnp.concatenate(
        [_scores(q_ref[:, h * QK_CAT:h * QK_CAT + LANE], wukv_ref[:, h * QK_NOPE:(h + 1) * QK_NOPE])
         for h in range(MLA_HEADS)], axis=0).astype(BF16)
    q_rope = jnp.concatenate([q_ref[:, h * QK_CAT + LANE:(h + 1) * QK_CAT] for h in range(MLA_HEADS)], axis=0)
    lat_p = latp_ref[...].astype(BF16)
    lat_n = latn_ref[...].astype(BF16)
    ks_p = jnp.dot(krp_ref[...].astype(BF16), pad_ref[...], preferred_element_type=F32).astype(BF16)
    ks_n = jnp.dot(krn_ref[...].astype(BF16), pad_ref[...], preferred_element_type=F32).astype(BF16)
    s_p = _scores(q_lat, lat_p) + _scores(q_rope, ks_p)
    s_n = _scores(q_lat, lat_n) + _scores(q_rope, ks_n)
    m = jnp.maximum(jnp.max(s_p, axis=-1, keepdims=True), jnp.max(s_n, axis=-1, keepdims=True))
    p_p = jnp.exp2(s_p - m)
    p_n = jnp.exp2(s_n - m)
    l = jnp.sum(p_p, axis=-1, keepdims=True) + jnp.sum(p_n, axis=-1, keepdims=True)
    ctx = (jnp.dot(p_p.astype(BF16), lat_p, preferred_element_type=F32)
           + jnp.dot(p_n.astype(BF16), lat_n, preferred_element_type=F32)) / l
    ctx = ctx.astype(BF16)
    for h in range(MLA_HEADS):
        o_ref[:, _v_cols(h)] = jnp.dot(ctx[h * t:(h + 1) * t, :], wukv_ref[:, nq + h * V_HEAD:nq + (h + 1) * V_HEAD],
                                       preferred_element_type=F32).astype(o_ref.dtype)


def attention_sample(qcat, lat_past, kr_past, lat_new, kr_new, w_ukv, bsz, t, past):
    v_w = MLA_HEADS * V_HEAD
    pad = jnp.eye(QK_ROPE, LANE, dtype=BF16)
    return pl.pallas_call(
        _attn_sample_body,
        grid=(bsz,),
        in_specs=[pl.BlockSpec((t, MLA_HEADS * QK_CAT), lambda b: (b, 0)),
                  pl.BlockSpec((past, KV_LORA), lambda b: (b, 0)),
                  pl.BlockSpec((past, QK_ROPE), lambda b: (b, 0)),
                  pl.BlockSpec((t, KV_LORA), lambda b: (b, 0)),
                  pl.BlockSpec((t, QK_ROPE), lambda b: (b, 0)),
                  _resident(pad), _resident(w_ukv)],
        out_specs=pl.BlockSpec((t, v_w), lambda b: (b, 0)),
        out_shape=jax.ShapeDtypeStruct((bsz * t, v_w), BF16),
        compiler_params=_cparams(("parallel",)),
        name="attention_sample",
    )(qcat, lat_past, kr_past, lat_new, kr_new, pad, w_ukv)


def _combine_body(ya_ref, ob_ref, pg_ref, x_ref, woa_ref, wob_ref, wout_ref, gffn_ref, h_ref, hn_ref):
    y_a = jnp.dot(ya_ref[...], woa_ref[...], preferred_element_type=F32)
    y_b = jnp.dot(ob_ref[...], wob_ref[...], preferred_element_type=F32)
    gate_a = jax.nn.sigmoid(pg_ref[:, 0:D_MODEL])
    gate_b = jax.nn.sigmoid(pg_ref[:, D_MODEL:2 * D_MODEL])
    mix = (gate_a * y_a + gate_b * y_b).astype(BF16)
    h = x_ref[...] + jnp.dot(mix, wout_ref[...], preferred_element_type=F32)
    h_ref[...] = h
    hn_ref[...] = _rms(h, gffn_ref[...]).astype(BF16)


def combine(ya, ob, pg, x, w_o_a, w_o_b, w_out, g_ffn, tm):
    n = x.shape[0]
    row = lambda w: pl.BlockSpec((tm, w), lambda i: (i, 0))
    res = lambda a: pl.BlockSpec(a.shape, lambda i: (0, 0), pipeline_mode=pl.Buffered(1))
    return pl.pallas_call(
        _combine_body,
        grid=(n // tm,),
        in_specs=[row(RW), row(MLA_HEADS * V_HEAD), row(2 * D_MODEL), row(D_MODEL),
                  res(w_o_a), res(w_o_b), res(w_out), res(g_ffn)],
        out_specs=[row(D_MODEL), row(D_MODEL)],
        out_shape=[jax.ShapeDtypeStruct((n, D_MODEL), F32), jax.ShapeDtypeStruct((n, D_MODEL), BF16)],
        compiler_params=_cparams(("parallel",)),
        name="combine",
    )(ya, ob, pg, x, w_o_a, w_o_b, w_out, g_ffn)


HALO = 16


def _ffn_in_body(hn_ref, halo_ref, wg_ref, wu_ref, cwb_ref, st_ref, act_ref, cnew_ref, wgb_ref, wub_ref, *,
                 tm, ts, tiles_per_seq):
    i = pl.program_id(1)
    nseq = tm // ts
    tn = act_ref.shape[1]

    @pl.when(i == 0)
    def _():
        wgb_ref[...] = wg_ref[...].astype(BF16)
        wub_ref[...] = wu_ref[...].astype(BF16)

    hn = hn_ref[...]
    t_in = lax.broadcasted_iota(jnp.int32, (tm, 1), 0) % ts
    gate = jnp.dot(hn, wgb_ref[...], preferred_element_type=F32)
    up = jnp.dot(hn, wub_ref[...], preferred_element_type=F32)
    if nseq == 1:
        prev2, prev1 = st_ref[0, 0:1, :], st_ref[0, 1:2, :]
    else:
        prev2 = jnp.broadcast_to(st_ref[:, 0:1, :], (nseq, ts, tn)).reshape(tm, tn)
        prev1 = jnp.broadcast_to(st_ref[:, 1:2, :], (nseq, ts, tn)).reshape(tm, tn)
    if tiles_per_seq > 1:
        first = (i % tiles_per_seq) == 0
        ghalo = jnp.dot(halo_ref[...], wgb_ref[...], preferred_element_type=F32)
        prev2 = jnp.where(first, prev2, ghalo[HALO - 2:HALO - 1, :])
        prev1 = jnp.where(first, prev1, ghalo[HALO - 1:HALO, :])
    g1 = jnp.where(t_in == 0, prev1, pltpu.roll(gate, 1, 0))
    g2 = jnp.where(t_in == 0, prev2, jnp.where(t_in == 1, prev1, pltpu.roll(gate, 2, 0)))
    gate_c = g2 * cwb_ref[0:1, :] + g1 * cwb_ref[1:2, :] + gate * cwb_ref[2:3, :] + cwb_ref[3:4, :]
    act_ref[...] = (jax.nn.silu(gate_c) * up).astype(act_ref.dtype)

    @pl.when((i % tiles_per_seq) == tiles_per_seq - 1)
    def _():
        cnew_ref[...] = gate.reshape(nseq, ts, tn)[:, ts - (CONV_W - 1):ts, :]


def ffn_in(hn, w_ffn_in, cwb, conv_state, bsz, t, tm, tn):
    n = bsz * t
    ts = min(t, tm)
    tps = t // ts
    nseq = tm // ts
    ncol = D_FF // tn
    r = tm // HALO
    return pl.pallas_call(
        functools.partial(_ffn_in_body, tm=tm, ts=ts, tiles_per_seq=tps),
        grid=(ncol, n // tm),
        in_specs=[pl.BlockSpec((tm, D_MODEL), lambda j, i: (i, 0)),
                  pl.BlockSpec((HALO, D_MODEL), lambda j, i: (jnp.maximum(i * r - 1, 0), 0)),
                  pl.BlockSpec((D_MODEL, tn), lambda j, i: (0, j)),
                  pl.BlockSpec((D_MODEL, tn), lambda j, i: (0, j + ncol)),
                  pl.BlockSpec((8, tn), lambda j, i: (0, j)),
                  pl.BlockSpec((nseq, 8, tn), lambda j, i: (i // tps, 0, j))],
        out_specs=[pl.BlockSpec((tm, tn), lambda j, i: (i, j)),
                   pl.BlockSpec((nseq, CONV_W - 1, tn), lambda j, i: (i // tps, 0, j))],
        out_shape=[jax.ShapeDtypeStruct((n, D_FF), BF16),
                   jax.ShapeDtypeStruct((bsz, CONV_W - 1, D_FF), F32)],
        scratch_shapes=[pltpu.VMEM((D_MODEL, tn), BF16), pltpu.VMEM((D_MODEL, tn), BF16)],
        compiler_params=_cparams(("parallel", "arbitrary")),
        name="ffn_in",
    )(hn, hn, w_ffn_in, w_ffn_in, cwb, conv_state)


def _ffn_down_body(act_ref, w_ref, h_ref, g_ref, y_ref):
    y_ref[...] = _rms(h_ref[...] + jnp.dot(act_ref[...], w_ref[...], preferred_element_type=F32), g_ref[...])


def ffn_down(act, w_down, h, g_final, tm):
    n = h.shape[0]
    return pl.pallas_call(
        _ffn_down_body,
        grid=(n // tm,),
        in_specs=[pl.BlockSpec((tm, D_FF), lambda i: (i, 0)), _resident(w_down),
                  pl.BlockSpec((tm, D_MODEL), lambda i: (i, 0)), _resident(g_final)],
        out_specs=pl.BlockSpec((tm, D_MODEL), lambda i: (i, 0)),
        out_shape=jax.ShapeDtypeStruct((n, D_MODEL), F32),
        compiler_params=_cparams(("parallel",)),
        name="ffn_down",
    )(act, w_down, h, g_final)


def _pad_cols(w, width):
    return jnp.pad(w, ((0, 0), (0, width - w.shape[1])))


def _prepare_params(g_mix, w_in, mu_shift, w0, w2, a0, a2, g2, k_k, k_a, r_k, lnx_g, lnx_b, w_o_a, g_kv,
                    w_ukv, w_o_b, w_out, g_ffn, w_ffn_in, conv_w, conv_b, w_ffn_down, g_final):
    o_zw = 3 * RW
    o_za = o_zw + DECAY_LORA
    o_zg = o_za + AAA_LORA
    o_q = RWKV_PROJ
    o_ckv = o_q + MLA_Q_WIDTH
    o_kr = o_ckv + KV_LORA
    o_g = RWKV_PROJ + MLA_PROJ

    def sect_a(m):
        return jnp.concatenate([m[:, :o_zw], _pad_cols(m[:, o_zw:o_za], LORA_PAD),
                                _pad_cols(m[:, o_za:o_zg], LORA_PAD), m[:, o_zg:RWKV_PROJ]], axis=1)

    wq = w_in[:, o_q:o_ckv].reshape(D_MODEL, MLA_HEADS, QK_NOPE + QK_ROPE)
    w_b = jnp.concatenate([wq[:, :, :QK_NOPE].reshape(D_MODEL, -1), wq[:, :, QK_NOPE:].reshape(D_MODEL, -1),
                           w_in[:, o_ckv:o_kr], _pad_cols(w_in[:, o_kr:o_g], LANE)], axis=1)
    wkv = w_ukv.reshape(KV_LORA, MLA_HEADS, QK_NOPE + V_HEAD)
    w_ukv_p = jnp.concatenate([wkv[:, :, :QK_NOPE].reshape(KV_LORA, -1),
                               wkv[:, :, QK_NOPE:].reshape(KV_LORA, -1)], axis=1)
    pad_rows = lambda m: jnp.pad(m, ((0, LORA_PAD - m.shape[0]), (0, 0)))
    row = lambda vct: vct.reshape(1, -1)
    cwb = jnp.concatenate([conv_w, conv_b.reshape(1, D_FF), jnp.zeros((8 - CONV_W - 1, D_FF), F32)], axis=0)
    return dict(
        g_mix=row(g_mix), w_a=sect_a(w_in[:, :RWKV_PROJ]).astype(BF16), w_b=w_b.astype(BF16),
        w_g=w_in[:, o_g:].astype(BF16), mu_a=sect_a(row(mu_shift)),
        w0=row(w0), w2=pad_rows(w2).astype(BF16), a0=row(a0), a2=pad_rows(a2).astype(BF16),
        g2=g2.astype(BF16), k_k=row(k_k), k_a=row(k_a), r_k=row(r_k.reshape(-1)),
        lnx_g=row(lnx_g), lnx_b=row(lnx_b), w_o_a=w_o_a.astype(BF16), g_kv=row(g_kv),
        w_ukv=w_ukv_p.astype(BF16), w_o_b=w_o_b.astype(BF16), w_out=w_out.astype(BF16),
        g_ffn=row(g_ffn), w_ffn_in=w_ffn_in, cwb=cwb, w_ffn_down=w_ffn_down.astype(BF16),
        g_final=row(g_final), sect_a=sect_a)


def _rope_tables(pos):
    half = QK_ROPE // 2
    inv = ROPE_THETA ** (-jnp.arange(half, dtype=F32) / half)
    ang = pos.astype(F32)[:, None] * inv[None, :]
    cos, sin = jnp.cos(ang), jnp.sin(ang)
    return jnp.tile(cos, (1, 4)), jnp.tile(jnp.concatenate([-sin, sin], axis=1), (1, 2))


def _tile_rows(n, pref):
    return pref if n % pref == 0 else n


def _layer(x, pos, shift_prev, wkv0, conv_prev, past, prm):
    bsz, t, _ = x.shape
    n = bsz * t
    x2 = x.reshape(n, D_MODEL)
    tm = _tile_rows(n, 256)

    pa = norm_matmul(x2, prm["g_mix"], prm["w_a"], _tile_rows(n, 512))

    shift_p = jnp.pad(prm["sect_a"](shift_prev.reshape(bsz, RWKV_PROJ)).reshape(bsz, 1, PA_W),
                      ((0, 0), (0, 7), (0, 0)))
    if t >= 256:
        ya, wkv_new, pg = rwkv_mix(pa, shift_p, wkv0, x2, prm, bsz, t, 64, 4)
    else:
        ya, wkv_new = rwkv_mix(pa, shift_p, wkv0, None, prm, bsz, t, min(t, 64), 1)
        pg = norm_matmul(x2, prm["g_mix"], prm["w_g"], tm)
    last = pa.reshape(bsz, t, PA_W)[:, t - 1]
    o_zw = 3 * RW
    shift_new = jnp.concatenate([last[:, :o_zw], last[:, o_zw:o_zw + DECAY_LORA],
                                 last[:, o_zw + LORA_PAD:o_zw + LORA_PAD + AAA_LORA],
                                 last[:, o_zw + 2 * LORA_PAD:]], axis=1).reshape(bsz, 1, RWKV_PROJ)

    cs, sn = _rope_tables(pos)
    if past is None:
        tmb = _tile_rows(t, 512)
    else:
        tmb = n
        cs, sn = jnp.tile(cs, (bsz, 1)), jnp.tile(sn, (bsz, 1))
    qcat, lat, kr, kcat, v = mla_prep(x2, prm["g_mix"], prm["w_b"], cs, sn, prm["g_kv"], prm["w_ukv"], tmb)
    if past is None:
        ob = attention_prompt(qcat, kcat, v, bsz, t, 512, 4)
    else:
        plat, pkr = past
        plen = plat.shape[1]
        ob = attention_sample(qcat, plat.reshape(bsz * plen, KV_LORA), pkr.reshape(bsz * plen, QK_ROPE),
                              lat, kr, prm["w_ukv"], bsz, t, plen)

    h, hn = combine(ya, ob, pg, x2, prm["w_o_a"], prm["w_o_b"], prm["w_out"], prm["g_ffn"], tm)
    conv_p = jnp.pad(conv_prev, ((0, 0), (0, 8 - (CONV_W - 1)), (0, 0)))
    tm_ffn = _tile_rows(n, 1024) if t >= 1024 else _tile_rows(n, 512)
    act, conv_new = ffn_in(hn, prm["w_ffn_in"], prm["cwb"], conv_p, bsz, t, tm_ffn, 512)
    y = ffn_down(act, prm["w_ffn_down"], h, prm["g_final"], tm)
    return (y.reshape(bsz, t, D_MODEL), lat.reshape(bsz, t, KV_LORA), kr.reshape(bsz, t, QK_ROPE),
            wkv_new, shift_new, conv_new)


def kernel(x_prompt, x_sample, cache_mla_latent, cache_mla_krope, state_rwkv_wkv, state_rwkv_shift, state_ffn_conv, g_mix, w_in, mu_shift, w0, w2, a0, a2, g2, k_k, k_a, r_k, lnx_g, lnx_b, w_o_a, g_kv, w_ukv, w_o_b, w_out, g_ffn, w_ffn_in, conv_w, conv_b, w_ffn_down, g_final):
    depth = w_in.shape[0]
    assert depth == 1, "single-layer step"
    bp, tp, _ = x_prompt.shape
    ts = x_sample.shape[1]
    past = cache_mla_latent.shape[2]
    prm = _prepare_params(g_mix[0], w_in[0], mu_shift[0], w0[0], w2[0], a0[0], a2[0], g2[0], k_k[0], k_a[0],
                          r_k[0], lnx_g[0], lnx_b[0], w_o_a[0], g_kv[0], w_ukv[0], w_o_b[0], w_out[0],
                          g_ffn[0], w_ffn_in[0], conv_w[0], conv_b[0], w_ffn_down[0], g_final)
    dt = x_prompt.dtype
    out_p = _layer(x_prompt, jnp.arange(tp), jnp.zeros((bp, 1, RWKV_PROJ), dt),
                   jnp.zeros((bp, N_HEADS, HEAD, HEAD), F32), jnp.zeros((bp, CONV_W - 1, D_FF), dt),
                   None, prm)
    out_s = _layer(x_sample, past + jnp.arange(ts), state_rwkv_shift[0], state_rwkv_wkv[0],
                   state_ffn_conv[0], (cache_mla_latent[0], cache_mla_krope[0]), prm)
    lead = lambda a: a[None].astype(dt)
    return (out_p[0], out_s[0],
            lead(out_p[1]), lead(out_p[2]), lead(out_p[3]), lead(out_p[4]), lead(out_p[5]),
            lead(out_s[1]), lead(out_s[2]), lead(out_s[3]), lead(out_s[4]), lead(out_s[5]))
```

```python
import functools
import math

import jax
import jax.numpy as jnp
from jax import lax
from jax.experimental import pallas as pl
from jax.experimental.pallas import tpu as pltpu

F32 = jnp.float32
BF16 = jnp.bfloat16

D_MODEL = 2048
NORM_EPS = 1e-6
HEAD = 64
N_HEADS = 16
RW = N_HEADS * HEAD
DECAY_LORA = 96
AAA_LORA = 96
GATE_LORA = 256
GN_EPS = 64e-5
RWKV_PROJ = 3 * RW + DECAY_LORA + AAA_LORA + GATE_LORA
MLA_HEADS = 8
QK_NOPE = 128
QK_ROPE = 64
V_HEAD = 128
KV_LORA = 512
ROPE_THETA = 10000.0
MLA_Q_WIDTH = MLA_HEADS * (QK_NOPE + QK_ROPE)
MLA_PROJ = MLA_Q_WIDTH + KV_LORA + QK_ROPE
SOFTMAX_SCALE = (QK_NOPE + QK_ROPE) ** -0.5
Q_SCALE = SOFTMAX_SCALE * math.log2(math.e)
NEG_INF = -1e30
CHUNK = 64
D_FF = 5632
CONV_W = 3

LANE = 128
LORA_PAD = 128
GATE_PIECE = 256
PA_W = 3 * RW + 2 * LORA_PAD + GATE_LORA
QR_W = MLA_HEADS * QK_ROPE
PB_W = MLA_HEADS * QK_NOPE + QR_W + KV_LORA + LANE
QK_CAT = 2 * LANE
VMEM_LIMIT = 56 * 1024 * 1024
RWKV_VMEM_LIMIT = 61 * 1024 * 1024


def _cparams(sem, vmem=VMEM_LIMIT):
    return pltpu.CompilerParams(dimension_semantics=sem, vmem_limit_bytes=vmem)


def _dot(a, b):
    return jnp.dot(a.astype(BF16), b.astype(BF16), preferred_element_type=F32)


def _dot_nt(a, b):
    return lax.dot_general(a.astype(BF16), b.astype(BF16), (((1,), (1,)), ((), ())),
                           preferred_element_type=F32)


def _dot_tn(a, b):
    return lax.dot_general(a.astype(BF16), b.astype(BF16), (((0,), (0,)), ((), ())),
                           preferred_element_type=F32)


def _rms(x, g):
    return x * lax.rsqrt(jnp.mean(x * x, axis=-1, keepdims=True) + NORM_EPS) * g


def _resident(a):
    nd = a.ndim
    return pl.BlockSpec(a.shape, lambda *_: (0,) * nd, pipeline_mode=pl.Buffered(1))


def _norm_matmul_body(x_ref, g_ref, w_ref, o_ref, *, parts):
    rows = x_ref.shape[0] // parts
    for p in range(parts):
        rs = slice(p * rows, (p + 1) * rows)
        xn = _rms(x_ref[rs, :], g_ref[...]).astype(BF16)
        o_ref[rs, :] = jnp.dot(xn, w_ref[...], preferred_element_type=F32).astype(o_ref.dtype)


def norm_matmul(x, g, w, tm, out_dtype=F32):
    n, k = x.shape
    m = w.shape[1]
    return pl.pallas_call(
        functools.partial(_norm_matmul_body, parts=2),
        grid=(n // tm,),
        in_specs=[pl.BlockSpec((tm, k), lambda i: (i, 0)), _resident(g), _resident(w)],
        out_specs=pl.BlockSpec((tm, m), lambda i: (i, 0)),
        out_shape=jax.ShapeDtypeStruct((n, m), out_dtype),
        compiler_params=_cparams(("parallel",)),
        name="norm_matmul",
    )(x, g, w)


def _rwkv_body(pa_ref, shift_ref, wkv0_ref, mu_ref, w0_ref, w2_ref, a0_ref, a2_ref, g2_ref,
               kk_ref, ka_ref, rk_ref, lng_ref, lnb_ref, *rest, c, nch, fuse_gate):
    if fuse_gate:
        x_ref, gmix_ref, wg_ref, y_ref, wkv_ref, pg_ref, st_ref, prev_ref = rest
    else:
        y_ref, wkv_ref, st_ref, prev_ref = rest
    ci = pl.program_id(1)
    nc = pl.num_programs(1)
    rows = nch * c

    @pl.when(ci == 0)
    def _():
        prev_ref[...] = shift_ref[0]
        st_ref[...] = jnp.zeros(st_ref.shape, F32)
        for h in range(N_HEADS):
            o = (h % 2) * HEAD
            st_ref[h // 2, o:o + HEAD, o:o + HEAD] = wkv0_ref[0, h].T

    pa = pa_ref[...]
    row = lax.broadcasted_iota(jnp.int32, (rows, 1), 0)
    p_prev = jnp.where(row == 0, prev_ref[0:1, :], pltpu.roll(pa, 1, 0))
    prev_ref[0:1, :] = pa[rows - 1:rows, :]
    z = pa + mu_ref[...] * (p_prev - pa)

    n_pieces = 2 * D_MODEL // GATE_PIECE if fuse_gate else 0
    pieces = iter(range(n_pieces))
    xn = _rms(x_ref[...], gmix_ref[...]).astype(BF16) if fuse_gate else None

    def gate_pieces(k):
        for _ in range(k):
            j = next(pieces, None)
            if j is not None:
                cs = slice(j * GATE_PIECE, (j + 1) * GATE_PIECE)
                pg_ref[:, cs] = jnp.dot(xn, wg_ref[:, cs], preferred_element_type=F32)

    gate_pieces(2)

    r_all = z[:, 0:RW]
    k_all = z[:, RW:2 * RW]
    v_all = z[:, 2 * RW:3 * RW]
    zw = z[:, 3 * RW:3 * RW + LORA_PAD]
    za = z[:, 3 * RW + LORA_PAD:3 * RW + 2 * LORA_PAD]
    zg = z[:, 3 * RW + 2 * LORA_PAD:PA_W]

    w_raw = w0_ref[...] + _dot(jnp.tanh(zw), w2_ref[...])
    logw = -jnp.exp(-jax.nn.softplus(-w_raw) - 0.5)
    a_all = jax.nn.sigmoid(a0_ref[...] + _dot(za, a2_ref[...]))
    g_all = _dot(jax.nn.sigmoid(zg), g2_ref[...])
    gate_pieces(1)
    kk_all = k_all * kk_ref[...]
    k2_all = k_all * (1.0 + (a_all - 1.0) * ka_ref[...])

    ti = lax.broadcasted_iota(jnp.int32, (rows, rows), 0)
    tj = lax.broadcasted_iota(jnp.int32, (rows, rows), 1)
    tri = ((ti >= tj) & (ti // c == tj // c)).astype(BF16)
    w_hi = logw.astype(BF16)
    w_mid = (logw - w_hi.astype(F32)).astype(BF16)
    w_lo = (logw - w_hi.astype(F32) - w_mid.astype(F32)).astype(BF16)
    cum_all = (jnp.dot(tri, w_hi, preferred_element_type=F32) + jnp.dot(tri, w_mid, preferred_element_type=F32)
               + jnp.dot(tri, w_lo, preferred_element_type=F32))

    c2 = 2 * c
    low = lax.broadcasted_iota(jnp.int32, (c, LANE), 1) < HEAD
    ri = lax.broadcasted_iota(jnp.int32, (c2, c2), 0)
    cj = lax.broadcasted_iota(jnp.int32, (c2, c2), 1)
    same = (ri // c) == (cj // c)
    strict = same & (ri > cj)
    incl = same & (ri >= cj)
    eye_p = (lax.broadcasted_iota(jnp.int32, (LANE, LANE), 0)
             == lax.broadcasted_iota(jnp.int32, (LANE, LANE), 1))

    def stack(x):
        return jnp.concatenate([jnp.where(low, x, 0.0), jnp.where(low, 0.0, x)], axis=0).astype(BF16)

    def head_sum(x):
        s_a = jnp.sum(jnp.where(low, x, 0.0), axis=-1, keepdims=True)
        s_b = jnp.sum(jnp.where(low, 0.0, x), axis=-1, keepdims=True)
        return jnp.where(low, s_a, s_b)

    npair = N_HEADS // 2
    items = [(q, p) for q in range(nch) for p in range(npair)]
    idx = range(len(items))
    blk = lambda arr, q, p: arr[q * c:(q + 1) * c, p * LANE:(p + 1) * LANE]
    cum = [blk(cum_all, q, p) for q, p in items]
    lw = [blk(logw, q, p) for q, p in items]
    r = [blk(r_all, q, p) for q, p in items]
    k2 = [blk(k2_all, q, p) for q, p in items]
    v = [blk(v_all, q, p) for q, p in items]
    kk = [blk(kk_all, q, p) for q, p in items]
    kk = [x * lax.rsqrt(jnp.maximum(head_sum(x * x), 1e-12)) for x in kk]
    gam = [jnp.exp(x) for x in cum]
    inv_gam = [jnp.exp(-x) for x in cum]
    gam_c = [jnp.exp(x[c - 1:c, :]) for x in cum]
    at = [-kk[i] * jnp.exp(cum[i] - lw[i]) for i in idx]
    bt = [kk[i] * blk(a_all, *items[i]) * inv_gam[i] for i in idx]
    kt = [k2[i] * inv_gam[i] for i in idx]
    rt = [r[i] * gam[i] for i in idx]
    s_at = [stack(x) for x in at]
    s_rt = [stack(x) for x in rt]
    s_v = [stack(x) for x in v]
    pm = [lax.dot_general(jnp.concatenate([s_at[i], s_rt[i]], axis=0),
                          jnp.concatenate([stack(bt[i]), stack(kt[i])], axis=0),
                          (((1,), (1,)), ((), ())), preferred_element_type=F32) for i in idx]
    gate_pieces(1)
    a_ab = [jnp.where(strict, x[0:c2, 0:c2], 0.0) for x in pm]
    a_ak = [jnp.where(strict, x[0:c2, c2:2 * c2], 0.0).astype(BF16) for x in pm]
    a_br = [jnp.where(incl, x[c2:2 * c2, 0:c2], 0.0).astype(BF16) for x in pm]
    a_kr = [jnp.where(incl, x[c2:2 * c2, c2:2 * c2], 0.0).astype(BF16) for x in pm]

    xinv = [jnp.where(ri == cj, 1.0, x) for x in a_ab]
    pw = [_dot(x, x) for x in a_ab]
    gate_pieces(1)
    steps = int(math.log2(c)) - 1
    for n in range(1, steps + 1):
        pw_bf = [x.astype(BF16) for x in pw]
        if n < steps:
            both = [jnp.dot(jnp.concatenate([pw_bf[i], xinv[i].astype(BF16)], axis=0), pw_bf[i],
                            preferred_element_type=F32) for i in idx]
            pw = [x[0:c2, :] for x in both]
            xinv = [xinv[i] + both[i][c2:2 * c2, :] for i in idx]
            gate_pieces(1)
        else:
            xinv = [xinv[i] + jnp.dot(xinv[i].astype(BF16), pw_bf[i], preferred_element_type=F32) for i in idx]

    gate_pieces(1)
    akv = [jnp.dot(a_ak[i], s_v[i], preferred_element_type=F32) for i in idx]
    gate_pieces(1)
    sol = [_dot(xinv[i], jnp.concatenate([s_at[i], akv[i].astype(BF16)], axis=1)) for i in idx]
    gate_pieces(1)
    zero_slab = jnp.zeros((c2, LANE), BF16)
    rhs2 = [jnp.concatenate([sol[i].astype(BF16), jnp.concatenate([zero_slab, s_v[i]], axis=1)], axis=0)
            for i in idx]
    asol = [jnp.dot(jnp.concatenate([a_br[i], a_kr[i]], axis=1), rhs2[i], preferred_element_type=F32)
            for i in idx]
    gate_pieces(1)
    mn = [_dot_tn(jnp.concatenate([stack(bt[i] * gam_c[i]), stack(kt[i] * gam_c[i])], axis=0), rhs2[i])
          for i in idx]
    gate_pieces(1)
    rh_m = [jnp.concatenate([(s_rt[i] + asol[i][:, 0:LANE]).astype(BF16),
                             (jnp.where(eye_p, gam_c[i], 0.0) + mn[i][:, 0:LANE]).astype(BF16)], axis=0)
            for i in idx]
    n_mat = [mn[i][:, LANE:2 * LANE] for i in idx]
    y_add = [asol[i][:, LANE:2 * LANE] for i in idx]

    st = [st_ref[p] for p in range(npair)]
    y_st = []
    for q in range(nch):
        both = [jnp.dot(rh_m[q * npair + p], st[p].astype(BF16), preferred_element_type=F32)
                for p in range(npair)]
        y_st += [both[p][0:c2, :] + y_add[q * npair + p] for p in range(npair)]
        st = [both[p][c2:c2 + LANE, :] + n_mat[q * npair + p] for p in range(npair)]
    for p in range(npair):
        st_ref[p] = st[p]
    gate_pieces(n_pieces)

    outs = []
    for i in idx:
        q, p = items[i]
        ps = slice(p * LANE, (p + 1) * LANE)
        y = y_st[i][0:c, :] + y_st[i][c:c2, :]
        yc = y - head_sum(y) * (1.0 / HEAD)
        var = head_sum(yc * yc) * (1.0 / HEAD)
        yn = yc * lax.rsqrt(var + GN_EPS) * lng_ref[:, ps] + lnb_ref[:, ps]
        bonus = head_sum(r[i] * k2[i] * rk_ref[:, ps]) * v[i]
        outs.append((yn + bonus) * blk(g_all, q, p))
    y_ref[...] = jnp.concatenate(
        [jnp.concatenate(outs[q * npair:(q + 1) * npair], axis=1) for q in range(nch)], axis=0).astype(y_ref.dtype)

    @pl.when(ci == nc - 1)
    def _():
        for h in range(N_HEADS):
            o = (h % 2) * HEAD
            wkv_ref[0, h] = st_ref[h // 2, o:o + HEAD, o:o + HEAD].T


def rwkv_mix(pa, shift_prev, wkv0, x, prm, bsz, t, c, nch):
    rows = c * nch
    nc = t // rows
    fuse_gate = x is not None
    vec = lambda w: pl.BlockSpec((1, w), lambda b, i: (0, 0))
    mat = lambda r, w: pl.BlockSpec((r, w), lambda b, i: (0, 0))
    in_specs = [pl.BlockSpec((rows, PA_W), lambda b, i: (b * nc + i, 0)),
                pl.BlockSpec((1, 8, PA_W), lambda b, i: (b, 0, 0)),
                pl.BlockSpec((1, N_HEADS, HEAD, HEAD), lambda b, i: (b, 0, 0, 0)),
                vec(PA_W), vec(RW), mat(LORA_PAD, RW), vec(RW), mat(LORA_PAD, RW), mat(GATE_LORA, RW),
                vec(RW), vec(RW), vec(RW), vec(RW), vec(RW)]
    out_specs = [pl.BlockSpec((rows, RW), lambda b, i: (b * nc + i, 0)),
                 pl.BlockSpec((1, N_HEADS, HEAD, HEAD), lambda b, i: (b, 0, 0, 0))]
    out_shape = [jax.ShapeDtypeStruct((bsz * t, RW), BF16),
                 jax.ShapeDtypeStruct((bsz, N_HEADS, HEAD, HEAD), F32)]
    args = [pa, shift_prev, wkv0, prm["mu_a"], prm["w0"], prm["w2"], prm["a0"], prm["a2"], prm["g2"],
            prm["k_k"], prm["k_a"], prm["r_k"], prm["lnx_g"], prm["lnx_b"]]
    if fuse_gate:
        in_specs += [pl.BlockSpec((rows, D_MODEL), lambda b, i: (b * nc + i, 0)),
                     _resident(prm["g_mix"]), _resident(prm["w_g"])]
        out_specs.append(pl.BlockSpec((rows, 2 * D_MODEL), lambda b, i: (b * nc + i, 0)))
        out_shape.append(jax.ShapeDtypeStruct((bsz * t, 2 * D_MODEL), F32))
        args += [x, prm["g_mix"], prm["w_g"]]
    return pl.pallas_call(
        functools.partial(_rwkv_body, c=c, nch=nch, fuse_gate=fuse_gate),
        grid=(bsz, nc),
        in_specs=in_specs,
        out_specs=out_specs,
        out_shape=out_shape,
        scratch_shapes=[pltpu.VMEM((N_HEADS // 2, LANE, LANE), F32), pltpu.VMEM((8, PA_W), F32)],
        compiler_params=_cparams(("parallel", "arbitrary"), RWKV_VMEM_LIMIT),
        name="rwkv_mix",
    )(*args)


def _rope_swap(slab):
    lane = lax.broadcasted_iota(jnp.int32, slab.shape, 1)
    return jnp.where(lane % QK_ROPE < QK_ROPE // 2, pltpu.roll(slab, LANE - QK_ROPE // 2, 1),
                     pltpu.roll(slab, QK_ROPE // 2, 1))


def _expand_store(lat_bf, kslab_bf, wukv_ref, kcat_ref, v_ref):
    kv = jnp.dot(lat_bf, wukv_ref[...], preferred_element_type=F32)
    for h in range(MLA_HEADS):
        kcat_ref[:, h * QK_CAT:h * QK_CAT + LANE] = kv[:, h * QK_NOPE:(h + 1) * QK_NOPE].astype(BF16)
        kcat_ref[:, h * QK_CAT + LANE:(h + 1) * QK_CAT] = kslab_bf
    v_ref[...] = kv[:, MLA_HEADS * QK_NOPE:].astype(BF16)


def _mla_prep_body(x_ref, g_ref, wb_ref, cs_ref, sn_ref, gkv_ref, wukv_ref,
                   qcat_ref, lat_ref, kr_ref, kcat_ref, v_ref):
    xn = _rms(x_ref[...], g_ref[...]).astype(BF16)
    pb = jnp.dot(xn, wb_ref[...], preferred_element_type=F32)
    cs = cs_ref[...]
    sn = sn_ref[...]
    lane = lax.broadcasted_iota(jnp.int32, cs.shape, 1)
    low = lane < QK_ROPE
    nq = MLA_HEADS * QK_NOPE
    for p in range(MLA_HEADS // 2):
        slab = pb[:, nq + p * LANE:nq + (p + 1) * LANE]
        roped = (slab * cs + _rope_swap(slab) * sn) * Q_SCALE
        for j in range(2):
            h = 2 * p + j
            qn = pb[:, h * QK_NOPE:(h + 1) * QK_NOPE] * Q_SCALE
            qcat_ref[:, h * QK_CAT:h * QK_CAT + LANE] = qn.astype(BF16)
            half = roped if j == 0 else pltpu.roll(roped, QK_ROPE, 1)
            qcat_ref[:, h * QK_CAT + LANE:(h + 1) * QK_CAT] = jnp.where(low, half, 0.0).astype(BF16)
    lat = _rms(pb[:, nq + QR_W:nq + QR_W + KV_LORA], gkv_ref[...])
    lat_ref[...] = lat
    kslab = pb[:, nq + QR_W + KV_LORA:PB_W]
    kroped = jnp.where(low, kslab * cs + _rope_swap(kslab) * sn, 0.0)
    kr_ref[...] = kroped[:, 0:QK_ROPE]
    _expand_store(lat.astype(BF16), kroped.astype(BF16), wukv_ref, kcat_ref, v_ref)


def mla_prep(x, g_mix, w_b, cs, sn, g_kv, w_ukv, tm):
    n = x.shape[0]
    ntab = cs.shape[0] // tm
    full = lambda a: pl.BlockSpec(a.shape, lambda i: (0, 0))
    return pl.pallas_call(
        _mla_prep_body,
        grid=(n // tm,),
        in_specs=[pl.BlockSpec((tm, D_MODEL), lambda i: (i, 0)), full(g_mix), full(w_b),
                  pl.BlockSpec((tm, LANE), lambda i: (i % ntab, 0)),
                  pl.BlockSpec((tm, LANE), lambda i: (i % ntab, 0)), full(g_kv), full(w_ukv)],
        out_specs=[pl.BlockSpec((tm, MLA_HEADS * QK_CAT), lambda i: (i, 0)),
                   pl.BlockSpec((tm, KV_LORA), lambda i: (i, 0)),
                   pl.BlockSpec((tm, QK_ROPE), lambda i: (i, 0)),
                   pl.BlockSpec((tm, MLA_HEADS * QK_CAT), lambda i: (i, 0)),
                   pl.BlockSpec((tm, MLA_HEADS * V_HEAD), lambda i: (i, 0))],
        out_shape=[jax.ShapeDtypeStruct((n, MLA_HEADS * QK_CAT), BF16),
                   jax.ShapeDtypeStruct((n, KV_LORA), F32),
                   jax.ShapeDtypeStruct((n, QK_ROPE), F32),
                   jax.ShapeDtypeStruct((n, MLA_HEADS * QK_CAT), BF16),
                   jax.ShapeDtypeStruct((n, MLA_HEADS * V_HEAD), BF16)],
        compiler_params=_cparams(("parallel",)),
        name="mla_prep",
    )(x, g_mix, w_b, cs, sn, g_kv, w_ukv)


def _softmax_block(s, v, m, l, acc):
    m_new = jnp.maximum(m, jnp.max(s, axis=-1, keepdims=True))
    alpha = jnp.exp2(m - m_new)
    p = jnp.exp2(s - m_new)
    l_new = alpha * l + jnp.sum(p, axis=-1, keepdims=True)
    acc_new = alpha * acc + jnp.dot(p.astype(BF16), v, preferred_element_type=F32)
    return m_new, l_new, acc_new


def _scores(q, kblk):
    return lax.dot_general(q, kblk, (((1,), (1,)), ((), ())), preferred_element_type=F32)


def _qk_cols(h):
    return slice(h * QK_CAT, (h + 1) * QK_CAT)


def _v_cols(h):
    return slice(h * V_HEAD, (h + 1) * V_HEAD)


def _attn_prompt_body(q_ref, k_ref, v_ref, o_ref, *, tq, hb, nq):
    i = pl.program_id(2)
    qc = lax.broadcasted_iota(jnp.int32, (tq, tq), 0) // CHUNK
    kc = lax.broadcasted_iota(jnp.int32, (tq, tq), 1) // CHUNK
    visible = kc <= qc

    def run(ii):
        start = ii * tq
        carry = []
        for h in range(hb):
            s = jnp.where(visible, _scores(q_ref[:, _qk_cols(h)], k_ref[start:start + tq, _qk_cols(h)]), NEG_INF)
            m = jnp.max(s, axis=-1, keepdims=True)
            p = jnp.exp2(s - m)
            carry.append((m, jnp.sum(p, axis=-1, keepdims=True),
                          jnp.dot(p.astype(BF16), v_ref[start:start + tq, _v_cols(h)], preferred_element_type=F32)))
        for j in range(ii):
            off = j * tq
            carry = [_softmax_block(_scores(q_ref[:, _qk_cols(h)], k_ref[off:off + tq, _qk_cols(h)]),
                                    v_ref[off:off + tq, _v_cols(h)], *carry[h]) for h in range(hb)]
        for h in range(hb):
            m, l, acc = carry[h]
            o_ref[:, _v_cols(h)] = (acc / l).astype(o_ref.dtype)

    for ii in range(nq):
        pl.when(i == ii)(functools.partial(run, ii))


def attention_prompt(qcat, kcat, v, bsz, t, tq, hb):
    nq = t // tq
    return pl.pallas_call(
        functools.partial(_attn_prompt_body, tq=tq, hb=hb, nq=nq),
        grid=(bsz, MLA_HEADS // hb, nq),
        in_specs=[pl.BlockSpec((tq, hb * QK_CAT), lambda b, h, i: (b * nq + i, h)),
                  pl.BlockSpec((t, hb * QK_CAT), lambda b, h, i: (b, h)),
                  pl.BlockSpec((t, hb * V_HEAD), lambda b, h, i: (b, h))],
        out_specs=pl.BlockSpec((tq, hb * V_HEAD), lambda b, h, i: (b * nq + i, h)),
        out_shape=jax.ShapeDtypeStruct((bsz * t, MLA_HEADS * V_HEAD), BF16),
        compiler_params=_cparams(("parallel", "parallel", "arbitrary")),
        name="attention_prompt",
    )(qcat, kcat, v)


def _attn_sample_body(q_ref, latp_ref, krp_ref, latn_ref, krn_ref, pad_ref, wukv_ref, o_ref):
    t = q_ref.shape[0]
    nq = MLA_HEADS * QK_NOPE
    q_lat = jnp.concatenate(
        [_scores(q_ref[:, h * QK_CAT:h * QK_CAT + LANE], wukv_ref[:, h * QK_NOPE:(h + 1) * QK_NOPE])
         for h in range(MLA_HEADS)], axis=0).astype(BF16)
    q_rope = jnp.concatenate([q_ref[:, h * QK_CAT + LANE:(h + 1) * QK_CAT] for h in range(MLA_HEADS)], axis=0)
    lat_p = latp_ref[...].astype(BF16)
    lat_n = latn_ref[...].astype(BF16)
    ks_p = jnp.dot(krp_ref[...].astype(BF16), pad_ref[...], preferred_element_type=F32).astype(BF16)
    ks_n = jnp.dot(krn_ref[...].astype(BF16), pad_ref[...], preferred_element_type=F32).astype(BF16)
    s_p = _scores(q_lat, lat_p) + _scores(q_rope, ks_p)
    s_n = _scores(q_lat, lat_n) + _scores(q_rope, ks_n)
    m = jnp.maximum(jnp.max(s_p, axis=-1, keepdims=True), jnp.max(s_n, axis=-1, keepdims=True))
    p_p = jnp.exp2(s_p - m)
    p_n = jnp.exp2(s_n - m)
    l = jnp.sum(p_p, axis=-1, keepdims=True) + jnp.sum(p_n, axis=-1, keepdims=True)
    ctx = (jnp.dot(p_p.astype(BF16), lat_p, preferred_element_type=F32)
           + jnp.dot(p_n.astype(BF16), lat_n, preferred_element_type=F32)) / l
    ctx = ctx.astype(BF16)
    for h in range(MLA_HEADS):
        o_ref[:, _v_cols(h)] = jnp.dot(ctx[h * t:(h + 1) * t, :], wukv_ref[:, nq + h * V_HEAD:nq + (h + 1) * V_HEAD],
                                       preferred_element_type=F32).astype(o_ref.dtype)


def attention_sample(qcat, lat_past, kr_past, lat_new, kr_new, w_ukv, bsz, t, past):
    v_w = MLA_HEADS * V_HEAD
    pad = jnp.eye(QK_ROPE, LANE, dtype=BF16)
    return pl.pallas_call(
        _attn_sample_body,
        grid=(bsz,),
        in_specs=[pl.BlockSpec((t, MLA_HEADS * QK_CAT), lambda b: (b, 0)),
                  pl.BlockSpec((past, KV_LORA), lambda b: (b, 0)),
                  pl.BlockSpec((past, QK_ROPE), lambda b: (b, 0)),
                  pl.BlockSpec((t, KV_LORA), lambda b: (b, 0)),
                  pl.BlockSpec((t, QK_ROPE), lambda b: (b, 0)),
                  _resident(pad), _resident(w_ukv)],
        out_specs=pl.BlockSpec((t, v_w), lambda b: (b, 0)),
        out_shape=jax.ShapeDtypeStruct((bsz * t, v_w), BF16),
        compiler_params=_cparams(("parallel",)),
        name="attention_sample",
    )(qcat, lat_past, kr_past, lat_new, kr_new, pad, w_ukv)


def _combine_body(ya_ref, ob_ref, pg_ref, x_ref, woa_ref, wob_ref, wout_ref, gffn_ref, h_ref, hn_ref):
    y_a = jnp.dot(ya_ref[...], woa_ref[...], preferred_element_type=F32)
    y_b = jnp.dot(ob_ref[...], wob_ref[...], preferred_element_type=F32)
    gate_a = jax.nn.sigmoid(pg_ref[:, 0:D_MODEL])
    gate_b = jax.nn.sigmoid(pg_ref[:, D_MODEL:2 * D_MODEL])
    mix = (gate_a * y_a + gate_b * y_b).astype(BF16)
    h = x_ref[...] + jnp.dot(mix, wout_ref[...], preferred_element_type=F32)
    h_ref[...] = h
    hn_ref[...] = _rms(h, gffn_ref[...]).astype(BF16)


def combine(ya, ob, pg, x, w_o_a, w_o_b, w_out, g_ffn, tm):
    n = x.shape[0]
    row = lambda w: pl.BlockSpec((tm, w), lambda i: (i, 0))
    res = lambda a: pl.BlockSpec(a.shape, lambda i: (0, 0), pipeline_mode=pl.Buffered(1))
    return pl.pallas_call(
        _combine_body,
        grid=(n // tm,),
        in_specs=[row(RW), row(MLA_HEADS * V_HEAD), row(2 * D_MODEL), row(D_MODEL),
                  res(w_o_a), res(w_o_b), res(w_out), res(g_ffn)],
        out_specs=[row(D_MODEL), row(D_MODEL)],
        out_shape=[jax.ShapeDtypeStruct((n, D_MODEL), F32), jax.ShapeDtypeStruct((n, D_MODEL), BF16)],
        compiler_params=_cparams(("parallel",)),
        name="combine",
    )(ya, ob, pg, x, w_o_a, w_o_b, w_out, g_ffn)


HALO = 16


def _ffn_in_body(hn_ref, halo_ref, wg_ref, wu_ref, cwb_ref, st_ref, act_ref, cnew_ref, wgb_ref, wub_ref, *,
                 tm, ts, tiles_per_seq):
    i = pl.program_id(1)
    nseq = tm // ts
    tn = act_ref.shape[1]

    @pl.when(i == 0)
    def _():
        wgb_ref[...] = wg_ref[...].astype(BF16)
        wub_ref[...] = wu_ref[...].astype(BF16)

    hn = hn_ref[...]
    t_in = lax.broadcasted_iota(jnp.int32, (tm, 1), 0) % ts
    gate = jnp.dot(hn, wgb_ref[...], preferred_element_type=F32)
    up = jnp.dot(hn, wub_ref[...], preferred_element_type=F32)
    if nseq == 1:
        prev2, prev1 = st_ref[0, 0:1, :], st_ref[0, 1:2, :]
    else:
        prev2 = jnp.broadcast_to(st_ref[:, 0:1, :], (nseq, ts, tn)).reshape(tm, tn)
        prev1 = jnp.broadcast_to(st_ref[:, 1:2, :], (nseq, ts, tn)).reshape(tm, tn)
    if tiles_per_seq > 1:
        first = (i % tiles_per_seq) == 0
        ghalo = jnp.dot(halo_ref[...], wgb_ref[...], preferred_element_type=F32)
        prev2 = jnp.where(first, prev2, ghalo[HALO - 2:HALO - 1, :])
        prev1 = jnp.where(first, prev1, ghalo[HALO - 1:HALO, :])
    g1 = jnp.where(t_in == 0, prev1, pltpu.roll(gate, 1, 0))
    g2 = jnp.where(t_in == 0, prev2, jnp.where(t_in == 1, prev1, pltpu.roll(gate, 2, 0)))
    gate_c = g2 * cwb_ref[0:1, :] + g1 * cwb_ref[1:2, :] + gate * cwb_ref[2:3, :] + cwb_ref[3:4, :]
    act_ref[...] = (jax.nn.silu(gate_c) * up).astype(act_ref.dtype)

    @pl.when((i % tiles_per_seq) == tiles_per_seq - 1)
    def _():
        cnew_ref[...] = gate.reshape(nseq, ts, tn)[:, ts - (CONV_W - 1):ts, :]


def ffn_in(hn, w_ffn_in, cwb, conv_state, bsz, t, tm, tn):
    n = bsz * t
    ts = min(t, tm)
    tps = t // ts
    nseq = tm // ts
    ncol = D_FF // tn
    r = tm // HALO
    return pl.pallas_call(
        functools.partial(_ffn_in_body, tm=tm, ts=ts, tiles_per_seq=tps),
        grid=(ncol, n // tm),
        in_specs=[pl.BlockSpec((tm, D_MODEL), lambda j, i: (i, 0)),
                  pl.BlockSpec((HALO, D_MODEL), lambda j, i: (jnp.maximum(i * r - 1, 0), 0)),
                  pl.BlockSpec((D_MODEL, tn), lambda j, i: (0, j)),
                  pl.BlockSpec((D_MODEL, tn), lambda j, i: (0, j + ncol)),
                  pl.BlockSpec((8, tn), lambda j, i: (0, j)),
                  pl.BlockSpec((nseq, 8, tn), lambda j, i: (i // tps, 0, j))],
        out_specs=[pl.BlockSpec((tm, tn), lambda j, i: (i, j)),
                   pl.BlockSpec((nseq, CONV_W - 1, tn), lambda j, i: (i // tps, 0, j))],
        out_shape=[jax.ShapeDtypeStruct((n, D_FF), BF16),
                   jax.ShapeDtypeStruct((bsz, CONV_W - 1, D_FF), F32)],
        scratch_shapes=[pltpu.VMEM((D_MODEL, tn), BF16), pltpu.VMEM((D_MODEL, tn), BF16)],
        compiler_params=_cparams(("parallel", "arbitrary")),
        name="ffn_in",
    )(hn, hn, w_ffn_in, w_ffn_in, cwb, conv_state)


def _ffn_down_body(act_ref, w_ref, h_ref, g_ref, y_ref):
    y_ref[...] = _rms(h_ref[...] + jnp.dot(act_ref[...], w_ref[...], preferred_element_type=F32), g_ref[...])


def ffn_down(act, w_down, h, g_final, tm):
    n = h.shape[0]
    return pl.pallas_call(
        _ffn_down_body,
        grid=(n // tm,),
        in_specs=[pl.BlockSpec((tm, D_FF), lambda i: (i, 0)), _resident(w_down),
                  pl.BlockSpec((tm, D_MODEL), lambda i: (i, 0)), _resident(g_final)],
        out_specs=pl.BlockSpec((tm, D_MODEL), lambda i: (i, 0)),
        out_shape=jax.ShapeDtypeStruct((n, D_MODEL), F32),
        compiler_params=_cparams(("parallel",)),
        name="ffn_down",
    )(act, w_down, h, g_final)


def _pad_cols(w, width):
    return jnp.pad(w, ((0, 0), (0, width - w.shape[1])))


def _prepare_params(g_mix, w_in, mu_shift, w0, w2, a0, a2, g2, k_k, k_a, r_k, lnx_g, lnx_b, w_o_a, g_kv,
                    w_ukv, w_o_b, w_out, g_ffn, w_ffn_in, conv_w, conv_b, w_ffn_down, g_final):
    o_zw = 3 * RW
    o_za = o_zw + DECAY_LORA
    o_zg = o_za + AAA_LORA
    o_q = RWKV_PROJ
    o_ckv = o_q + MLA_Q_WIDTH
    o_kr = o_ckv + KV_LORA
    o_g = RWKV_PROJ + MLA_PROJ

    def sect_a(m):
        return jnp.concatenate([m[:, :o_zw], _pad_cols(m[:, o_zw:o_za], LORA_PAD),
                                _pad_cols(m[:, o_za:o_zg], LORA_PAD), m[:, o_zg:RWKV_PROJ]], axis=1)

    wq = w_in[:, o_q:o_ckv].reshape(D_MODEL, MLA_HEADS, QK_NOPE + QK_ROPE)
    w_b = jnp.concatenate([wq[:, :, :QK_NOPE].reshape(D_MODEL, -1), wq[:, :, QK_NOPE:].reshape(D_MODEL, -1),
                           w_in[:, o_ckv:o_kr], _pad_cols(w_in[:, o_kr:o_g], LANE)], axis=1)
    wkv = w_ukv.reshape(KV_LORA, MLA_HEADS, QK_NOPE + V_HEAD)
    w_ukv_p = jnp.concatenate([wkv[:, :, :QK_NOPE].reshape(KV_LORA, -1),
                               wkv[:, :, QK_NOPE:].reshape(KV_LORA, -1)], axis=1)
    pad_rows = lambda m: jnp.pad(m, ((0, LORA_PAD - m.shape[0]), (0, 0)))
    row = lambda vct: vct.reshape(1, -1)
    cwb = jnp.concatenate([conv_w, conv_b.reshape(1, D_FF), jnp.zeros((8 - CONV_W - 1, D_FF), F32)], axis=0)
    return dict(
        g_mix=row(g_mix), w_a=sect_a(w_in[:, :RWKV_PROJ]).astype(BF16), w_b=w_b.astype(BF16),
        w_g=w_in[:, o_g:].astype(BF16), mu_a=sect_a(row(mu_shift)),
        w0=row(w0), w2=pad_rows(w2).astype(BF16), a0=row(a0), a2=pad_rows(a2).astype(BF16),
        g2=g2.astype(BF16), k_k=row(k_k), k_a=row(k_a), r_k=row(r_k.reshape(-1)),
        lnx_g=row(lnx_g), lnx_b=row(lnx_b), w_o_a=w_o_a.astype(BF16), g_kv=row(g_kv),
        w_ukv=w_ukv_p.astype(BF16), w_o_b=w_o_b.astype(BF16), w_out=w_out.astype(BF16),
        g_ffn=row(g_ffn), w_ffn_in=w_ffn_in, cwb=cwb, w_ffn_down=w_ffn_down.astype(BF16),
        g_final=row(g_final), sect_a=sect_a)


def _rope_tables(pos):
    half = QK_ROPE // 2
    inv = ROPE_THETA ** (-jnp.arange(half, dtype=F32) / half)
    ang = pos.astype(F32)[:, None] * inv[None, :]
    cos, sin = jnp.cos(ang), jnp.sin(ang)
    return jnp.tile(cos, (1, 4)), jnp.tile(jnp.concatenate([-sin, sin], axis=1), (1, 2))


def _tile_rows(n, pref):
    return pref if n % pref == 0 else n


def _layer(x, pos, shift_prev, wkv0, conv_prev, past, prm):
    bsz, t, _ = x.shape
    n = bsz * t
    x2 = x.reshape(n, D_MODEL)
    tm = _tile_rows(n, 256)

    pa = norm_matmul(x2, prm["g_mix"], prm["w_a"], _tile_rows(n, 512))

    shift_p = jnp.pad(prm["sect_a"](shift_prev.reshape(bsz, RWKV_PROJ)).reshape(bsz, 1, PA_W),
                      ((0, 0), (0, 7), (0, 0)))
    if t >= 256:
        ya, wkv_new, pg = rwkv_mix(pa, shift_p, wkv0, x2, prm, bsz, t, 64, 4)
    else:
        ya, wkv_new = rwkv_mix(pa, shift_p, wkv0, None, prm, bsz, t, min(t, 64), 1)
        pg = norm_matmul(x2, prm["g_mix"], prm["w_g"], tm)
    last = pa.reshape(bsz, t, PA_W)[:, t - 1]
    o_zw = 3 * RW
    shift_new = jnp.concatenate([last[:, :o_zw], last[:, o_zw:o_zw + DECAY_LORA],
                                 last[:, o_zw + LORA_PAD:o_zw + LORA_PAD + AAA_LORA],
                                 last[:, o_zw + 2 * LORA_PAD:]], axis=1).reshape(bsz, 1, RWKV_PROJ)

    cs, sn = _rope_tables(pos)
    if past is None:
        tmb = _tile_rows(t, 512)
    else:
        tmb = n
        cs, sn = jnp.tile(cs, (bsz, 1)), jnp.tile(sn, (bsz, 1))
    qcat, lat, kr, kcat, v = mla_prep(x2, prm["g_mix"], prm["w_b"], cs, sn, prm["g_kv"], prm["w_ukv"], tmb)
    if past is None:
        ob = attention_prompt(qcat, kcat, v, bsz, t, 512, 4)
    else:
        plat, pkr = past
        plen = plat.shape[1]
        ob = attention_sample(qcat, plat.reshape(bsz * plen, KV_LORA), pkr.reshape(bsz * plen, QK_ROPE),
                              lat, kr, prm["w_ukv"], bsz, t, plen)

    h, hn = combine(ya, ob, pg, x2, prm["w_o_a"], prm["w_o_b"], prm["w_out"], prm["g_ffn"], tm)
    conv_p = jnp.pad(conv_prev, ((0, 0), (0, 8 - (CONV_W - 1)), (0, 0)))
    tm_ffn = _tile_rows(n, 1024) if t >= 1024 else _tile_rows(n, 512)
    act, conv_new = ffn_in(hn, prm["w_ffn_in"], prm["cwb"], conv_p, bsz, t, tm_ffn, 512)
    y = ffn_down(act, prm["w_ffn_down"], h, prm["g_final"], tm)
    return (y.reshape(bsz, t, D_MODEL), lat.reshape(bsz, t, KV_LORA), kr.reshape(bsz, t, QK_ROPE),
            wkv_new, shift_new, conv_new)


def kernel(x_prompt, x_sample, cache_mla_latent, cache_mla_krope, state_rwkv_wkv, state_rwkv_shift, state_ffn_conv, g_mix, w_in, mu_shift, w0, w2, a0, a2, g2, k_k, k_a, r_k, lnx_g, lnx_b, w_o_a, g_kv, w_ukv, w_o_b, w_out, g_ffn, w_ffn_in, conv_w, conv_b, w_ffn_down, g_final):
    depth = w_in.shape[0]
    assert depth == 1, "single-layer step"
    bp, tp, _ = x_prompt.shape
    ts = x_sample.shape[1]
    past = cache_mla_latent.shape[2]
    prm = _prepare_params(g_mix[0], w_in[0], mu_shift[0], w0[0], w2[0], a0[0], a2[0], g2[0], k_k[0], k_a[0],
                          r_k[0], lnx_g[0], lnx_b[0], w_o_a[0], g_kv[0], w_ukv[0], w_o_b[0], w_out[0],
                          g_ffn[0], w_ffn_in[0], conv_w[0], conv_b[0], w_ffn_down[0], g_final)
    dt = x_prompt.dtype
    out_p = _layer(x_prompt, jnp.arange(tp), jnp.zeros((bp, 1, RWKV_PROJ), dt),
                   jnp.zeros((bp, N_HEADS, HEAD, HEAD), F32), jnp.zeros((bp, CONV_W - 1, D_FF), dt),
                   None, prm)
    out_s = _layer(x_sample, past + jnp.arange(ts), state_rwkv_shift[0], state_rwkv_wkv[0],
                   state_ffn_conv[0], (cache_mla_latent[0], cache_mla_krope[0]), prm)
    lead = lambda a: a[None].astype(dt)
    return (out_p[0], out_s[0],
            lead(out_p[1]), lead(out_p[2]), lead(out_p[3]), lead(out_p[4]), lead(out_p[5]),
            lead(out_s[1]), lead(out_s[2]), lead(out_s[3]), lead(out_s[4]), lead(out_s[5]))
```

```python
import functools
import math

import jax
import jax.numpy as jnp
from jax import lax
from jax.experimental import pallas as pl
from jax.experimental.pallas import tpu as pltpu

F32 = jnp.float32
BF16 = jnp.bfloat16

D_MODEL = 2048
NORM_EPS = 1e-6
HEAD = 64
N_HEADS = 16
RW = N_HEADS * HEAD
DECAY_LORA = 96
AAA_LORA = 96
GATE_LORA = 256
GN_EPS = 64e-5
RWKV_PROJ = 3 * RW + DECAY_LORA + AAA_LORA + GATE_LORA
MLA_HEADS = 8
QK_NOPE = 128
QK_ROPE = 64
V_HEAD = 128
KV_LORA = 512
ROPE_THETA = 10000.0
MLA_Q_WIDTH = MLA_HEADS * (QK_NOPE + QK_ROPE)
MLA_PROJ = MLA_Q_WIDTH + KV_LORA + QK_ROPE
SOFTMAX_SCALE = (QK_NOPE + QK_ROPE) ** -0.5
Q_SCALE = SOFTMAX_SCALE * math.log2(math.e)
NEG_INF = -1e30
CHUNK = 64
D_FF = 5632
CONV_W = 3

LANE = 128
LORA_PAD = 128
GATE_PIECE = 256
PA_W = 3 * RW + 2 * LORA_PAD + GATE_LORA
QR_W = MLA_HEADS * QK_ROPE
PB_W = MLA_HEADS * QK_NOPE + QR_W + KV_LORA + LANE
QK_CAT = 2 * LANE
VMEM_LIMIT = 56 * 1024 * 1024
RWKV_VMEM_LIMIT = 61 * 1024 * 1024


def _cparams(sem, vmem=VMEM_LIMIT):
    return pltpu.CompilerParams(dimension_semantics=sem, vmem_limit_bytes=vmem)


def _dot(a, b):
    return jnp.dot(a.astype(BF16), b.astype(BF16), preferred_element_type=F32)


def _dot_nt(a, b):
    return lax.dot_general(a.astype(BF16), b.astype(BF16), (((1,), (1,)), ((), ())),
                           preferred_element_type=F32)


def _dot_tn(a, b):
    return lax.dot_general(a.astype(BF16), b.astype(BF16), (((0,), (0,)), ((), ())),
                           preferred_element_type=F32)


def _rms(x, g):
    return x * lax.rsqrt(jnp.mean(x * x, axis=-1, keepdims=True) + NORM_EPS) * g


def _resident(a):
    nd = a.ndim
    return pl.BlockSpec(a.shape, lambda *_: (0,) * nd, pipeline_mode=pl.Buffered(1))


def _norm_matmul_body(x_ref, g_ref, w_ref, o_ref, *, parts):
    rows = x_ref.shape[0] // parts
    for p in range(parts):
        rs = slice(p * rows, (p + 1) * rows)
        xn = _rms(x_ref[rs, :], g_ref[...]).astype(BF16)
        o_ref[rs, :] = jnp.dot(xn, w_ref[...], preferred_element_type=F32).astype(o_ref.dtype)


def norm_matmul(x, g, w, tm, out_dtype=F32):
    n, k = x.shape
    m = w.shape[1]
    return pl.pallas_call(
        functools.partial(_norm_matmul_body, parts=2),
        grid=(n // tm,),
        in_specs=[pl.BlockSpec((tm, k), lambda i: (i, 0)), _resident(g), _resident(w)],
        out_specs=pl.BlockSpec((tm, m), lambda i: (i, 0)),
        out_shape=jax.ShapeDtypeStruct((n, m), out_dtype),
        compiler_params=_cparams(("parallel",)),
        name="norm_matmul",
    )(x, g, w)


def _rwkv_body(pa_ref, shift_ref, wkv0_ref, mu_ref, w0_ref, w2_ref, a0_ref, a2_ref, g2_ref,
               kk_ref, ka_ref, rk_ref, lng_ref, lnb_ref, *rest, c, nch, fuse_gate, separate):
    if fuse_gate:
        x_ref, gmix_ref, wg_ref, y_ref, wkv_ref, pg_ref, st_ref, prev_ref = rest
    else:
        y_ref, wkv_ref, st_ref, prev_ref = rest
    ci = pl.program_id(1)
    nc = pl.num_programs(1)
    rows = nch * c

    npair = N_HEADS // 2
    nseq = nch if separate else 1

    @pl.when(ci == 0)
    def _():
        prev_ref[...] = shift_ref[0]
        st_ref[...] = jnp.zeros(st_ref.shape, F32)
        for q in range(nseq):
            for h in range(N_HEADS):
                o = (h % 2) * HEAD
                st_ref[q * npair + h // 2, o:o + HEAD, o:o + HEAD] = wkv0_ref[q, h].T

    pa = pa_ref[...]
    row = lax.broadcasted_iota(jnp.int32, (rows, 1), 0)
    if separate:
        first_rows = jnp.broadcast_to(shift_ref[:, 0:1, :], (nch, c, PA_W)).reshape(rows, PA_W)
        p_prev = jnp.where(row % c == 0, first_rows, pltpu.roll(pa, 1, 0))
    else:
        p_prev = jnp.where(row == 0, prev_ref[0:1, :], pltpu.roll(pa, 1, 0))
        prev_ref[0:1, :] = pa[rows - 1:rows, :]
    z = pa + mu_ref[...] * (p_prev - pa)

    n_pieces = 2 * D_MODEL // GATE_PIECE if fuse_gate else 0
    pieces = iter(range(n_pieces))
    xn = _rms(x_ref[...], gmix_ref[...]).astype(BF16) if fuse_gate else None

    def gate_pieces(k):
        for _ in range(k):
            j = next(pieces, None)
            if j is not None:
                cs = slice(j * GATE_PIECE, (j + 1) * GATE_PIECE)
                pg_ref[:, cs] = jnp.dot(xn, wg_ref[:, cs], preferred_element_type=F32)

    gate_pieces(2)

    r_all = z[:, 0:RW]
    k_all = z[:, RW:2 * RW]
    v_all = z[:, 2 * RW:3 * RW]
    zw = z[:, 3 * RW:3 * RW + LORA_PAD]
    za = z[:, 3 * RW + LORA_PAD:3 * RW + 2 * LORA_PAD]
    zg = z[:, 3 * RW + 2 * LORA_PAD:PA_W]

    w_raw = w0_ref[...] + _dot(jnp.tanh(zw), w2_ref[...])
    logw = -jnp.exp(-jax.nn.softplus(-w_raw) - 0.5)
    a_all = jax.nn.sigmoid(a0_ref[...] + _dot(za, a2_ref[...]))
    g_all = _dot(jax.nn.sigmoid(zg), g2_ref[...])
    gate_pieces(1)
    kk_all = k_all * kk_ref[...]
    k2_all = k_all * (1.0 + (a_all - 1.0) * ka_ref[...])

    ti = lax.broadcasted_iota(jnp.int32, (rows, rows), 0)
    tj = lax.broadcasted_iota(jnp.int32, (rows, rows), 1)
    tri = ((ti >= tj) & (ti // c == tj // c)).astype(BF16)
    w_hi = logw.astype(BF16)
    w_mid = (logw - w_hi.astype(F32)).astype(BF16)
    w_lo = (logw - w_hi.astype(F32) - w_mid.astype(F32)).astype(BF16)
    cum_all = (jnp.dot(tri, w_hi, preferred_element_type=F32) + jnp.dot(tri, w_mid, preferred_element_type=F32)
               + jnp.dot(tri, w_lo, preferred_element_type=F32))

    c2 = 2 * c
    low = lax.broadcasted_iota(jnp.int32, (c, LANE), 1) < HEAD
    ri = lax.broadcasted_iota(jnp.int32, (c2, c2), 0)
    cj = lax.broadcasted_iota(jnp.int32, (c2, c2), 1)
    same = (ri // c) == (cj // c)
    strict = same & (ri > cj)
    incl = same & (ri >= cj)
    eye_p = (lax.broadcasted_iota(jnp.int32, (LANE, LANE), 0)
             == lax.broadcasted_iota(jnp.int32, (LANE, LANE), 1))

    def stack(x):
        return jnp.concatenate([jnp.where(low, x, 0.0), jnp.where(low, 0.0, x)], axis=0).astype(BF16)

    def head_sum(x):
        s_a = jnp.sum(jnp.where(low, x, 0.0), axis=-1, keepdims=True)
        s_b = jnp.sum(jnp.where(low, 0.0, x), axis=-1, keepdims=True)
        return jnp.where(low, s_a, s_b)

    items = [(q, p) for q in range(nch) for p in range(npair)]
    idx = range(len(items))
    blk = lambda arr, q, p: arr[q * c:(q + 1) * c, p * LANE:(p + 1) * LANE]
    cum = [blk(cum_all, q, p) for q, p in items]
    lw = [blk(logw, q, p) for q, p in items]
    r = [blk(r_all, q, p) for q, p in items]
    k2 = [blk(k2_all, q, p) for q, p in items]
    v = [blk(v_all, q, p) for q, p in items]
    kk = [blk(kk_all, q, p) for q, p in items]
    kk = [x * lax.rsqrt(jnp.maximum(head_sum(x * x), 1e-12)) for x in kk]
    gam = [jnp.exp(x) for x in cum]
    inv_gam = [jnp.exp(-x) for x in cum]
    gam_c = [jnp.exp(x[c - 1:c, :]) for x in cum]
    at = [-kk[i] * jnp.exp(cum[i] - lw[i]) for i in idx]
    bt = [kk[i] * blk(a_all, *items[i]) * inv_gam[i] for i in idx]
    kt = [k2[i] * inv_gam[i] for i in idx]
    rt = [r[i] * gam[i] for i in idx]
    s_at = [stack(x) for x in at]
    s_rt = [stack(x) for x in rt]
    s_v = [stack(x) for x in v]
    pm = [lax.dot_general(jnp.concatenate([s_at[i], s_rt[i]], axis=0),
                          jnp.concatenate([stack(bt[i]), stack(kt[i])], axis=0),
                          (((1,), (1,)), ((), ())), preferred_element_type=F32) for i in idx]
    gate_pieces(1)
    a_ab = [jnp.where(strict, x[0:c2, 0:c2], 0.0) for x in pm]
    a_ak = [jnp.where(strict, x[0:c2, c2:2 * c2], 0.0).astype(BF16) for x in pm]
    a_br = [jnp.where(incl, x[c2:2 * c2, 0:c2], 0.0).astype(BF16) for x in pm]
    a_kr = [jnp.where(incl, x[c2:2 * c2, c2:2 * c2], 0.0).astype(BF16) for x in pm]

    xinv = [jnp.where(ri == cj, 1.0, x) for x in a_ab]
    pw = [_dot(x, x) for x in a_ab]
    gate_pieces(1)
    steps = int(math.log2(c)) - 1
    for n in range(1, steps + 1):
        pw_bf = [x.astype(BF16) for x in pw]
        if n < steps:
            both = [jnp.dot(jnp.concatenate([pw_bf[i], xinv[i].astype(BF16)], axis=0), pw_bf[i],
                            preferred_element_type=F32) for i in idx]
            pw = [x[0:c2, :] for x in both]
            xinv = [xinv[i] + both[i][c2:2 * c2, :] for i in idx]
            gate_pieces(1)
        else:
            xinv = [xinv[i] + jnp.dot(xinv[i].astype(BF16), pw_bf[i], preferred_element_type=F32) for i in idx]

    gate_pieces(1)
    akv = [jnp.dot(a_ak[i], s_v[i], preferred_element_type=F32) for i in idx]
    gate_pieces(1)
    sol = [_dot(xinv[i], jnp.concatenate([s_at[i], akv[i].astype(BF16)], axis=1)) for i in idx]
    gate_pieces(1)
    zero_slab = jnp.zeros((c2, LANE), BF16)
    rhs2 = [jnp.concatenate([sol[i].astype(BF16), jnp.concatenate([zero_slab, s_v[i]], axis=1)], axis=0)
            for i in idx]
    asol = [jnp.dot(jnp.concatenate([a_br[i], a_kr[i]], axis=1), rhs2[i], preferred_element_type=F32)
            for i in idx]
    gate_pieces(1)
    mn = [_dot_tn(jnp.concatenate([stack(bt[i] * gam_c[i]), stack(kt[i] * gam_c[i])], axis=0), rhs2[i])
          for i in idx]
    gate_pieces(1)
    rh_m = [jnp.concatenate([(s_rt[i] + asol[i][:, 0:LANE]).astype(BF16),
                             (jnp.where(eye_p, gam_c[i], 0.0) + mn[i][:, 0:LANE]).astype(BF16)], axis=0)
            for i in idx]
    n_mat = [mn[i][:, LANE:2 * LANE] for i in idx]
    y_add = [asol[i][:, LANE:2 * LANE] for i in idx]

    if separate:
        both = [jnp.dot(rh_m[i], st_ref[i].astype(BF16), preferred_element_type=F32) for i in idx]
        y_st = [both[i][0:c2, :] + y_add[i] for i in idx]
        for i in idx:
            st_ref[i] = both[i][c2:c2 + LANE, :] + n_mat[i]
    else:
        st = [st_ref[p] for p in range(npair)]
        y_st = []
        for q in range(nch):
            both = [jnp.dot(rh_m[q * npair + p], st[p].astype(BF16), preferred_element_type=F32)
                    for p in range(npair)]
            y_st += [both[p][0:c2, :] + y_add[q * npair + p] for p in range(npair)]
            st = [both[p][c2:c2 + LANE, :] + n_mat[q * npair + p] for p in range(npair)]
        for p in range(npair):
            st_ref[p] = st[p]
    gate_pieces(n_pieces)

    outs = []
    for i in idx:
        q, p = items[i]
        ps = slice(p * LANE, (p + 1) * LANE)
        y = y_st[i][0:c, :] + y_st[i][c:c2, :]
        yc = y - head_sum(y) * (1.0 / HEAD)
        var = head_sum(yc * yc) * (1.0 / HEAD)
        yn = yc * lax.rsqrt(var + GN_EPS) * lng_ref[:, ps] + lnb_ref[:, ps]
        bonus = head_sum(r[i] * k2[i] * rk_ref[:, ps]) * v[i]
        outs.append((yn + bonus) * blk(g_all, q, p))
    y_ref[...] = jnp.concatenate(
        [jnp.concatenate(outs[q * npair:(q + 1) * npair], axis=1) for q in range(nch)], axis=0).astype(y_ref.dtype)

    @pl.when(ci == nc - 1)
    def _():
        for q in range(nseq):
            for h in range(N_HEADS):
                o = (h % 2) * HEAD
                wkv_ref[q, h] = st_ref[q * npair + h // 2, o:o + HEAD, o:o + HEAD].T


def rwkv_mix(pa, shift_prev, wkv0, x, prm, bsz, t, c, nch, separate=False):
    rows = c * nch
    nseq = nch if separate else 1
    nc = 1 if separate else t // rows
    fuse_gate = x is not None
    vec = lambda w: pl.BlockSpec((1, w), lambda b, i: (0, 0))
    mat = lambda r, w: pl.BlockSpec((r, w), lambda b, i: (0, 0))
    in_specs = [pl.BlockSpec((rows, PA_W), lambda b, i: (b * nc + i, 0)),
                pl.BlockSpec((nseq, 8, PA_W), lambda b, i: (b, 0, 0)),
                pl.BlockSpec((nseq, N_HEADS, HEAD, HEAD), lambda b, i: (b, 0, 0, 0)),
                vec(PA_W), vec(RW), mat(LORA_PAD, RW), vec(RW), mat(LORA_PAD, RW), mat(GATE_LORA, RW),
                vec(RW), vec(RW), vec(RW), vec(RW), vec(RW)]
    out_specs = [pl.BlockSpec((rows, RW), lambda b, i: (b * nc + i, 0)),
                 pl.BlockSpec((nseq, N_HEADS, HEAD, HEAD), lambda b, i: (b, 0, 0, 0))]
    out_shape = [jax.ShapeDtypeStruct((bsz * t, RW), BF16),
                 jax.ShapeDtypeStruct((bsz, N_HEADS, HEAD, HEAD), F32)]
    args = [pa, shift_prev, wkv0, prm["mu_a"], prm["w0"], prm["w2"], prm["a0"], prm["a2"], prm["g2"],
            prm["k_k"], prm["k_a"], prm["r_k"], prm["lnx_g"], prm["lnx_b"]]
    if fuse_gate:
        in_specs += [pl.BlockSpec((rows, D_MODEL), lambda b, i: (b * nc + i, 0)),
                     _resident(prm["g_mix"]), _resident(prm["w_g"])]
        out_specs.append(pl.BlockSpec((rows, 2 * D_MODEL), lambda b, i: (b * nc + i, 0)))
        out_shape.append(jax.ShapeDtypeStruct((bsz * t, 2 * D_MODEL), F32))
        args += [x, prm["g_mix"], prm["w_g"]]
    return pl.pallas_call(
        functools.partial(_rwkv_body, c=c, nch=nch, fuse_gate=fuse_gate, separate=separate),
        grid=(bsz // nseq, nc),
        in_specs=in_specs,
        out_specs=out_specs,
        out_shape=out_shape,
        scratch_shapes=[pltpu.VMEM((nseq * N_HEADS // 2, LANE, LANE), F32), pltpu.VMEM((8, PA_W), F32)],
        compiler_params=_cparams(("parallel", "arbitrary"), RWKV_VMEM_LIMIT),
        name="rwkv_mix",
    )(*args)


def _rope_swap(slab):
    lane = lax.broadcasted_iota(jnp.int32, slab.shape, 1)
    return jnp.where(lane % QK_ROPE < QK_ROPE // 2, pltpu.roll(slab, LANE - QK_ROPE // 2, 1),
                     pltpu.roll(slab, QK_ROPE // 2, 1))


def _expand_store(lat_bf, kslab_bf, wukv_ref, kcat_ref, v_ref):
    kv = jnp.dot(lat_bf, wukv_ref[...], preferred_element_type=F32)
    for h in range(MLA_HEADS):
        kcat_ref[:, h * QK_CAT:h * QK_CAT + LANE] = kv[:, h * QK_NOPE:(h + 1) * QK_NOPE].astype(BF16)
        kcat_ref[:, h * QK_CAT + LANE:(h + 1) * QK_CAT] = kslab_bf
    v_ref[...] = kv[:, MLA_HEADS * QK_NOPE:].astype(BF16)


def _mla_prep_body(x_ref, g_ref, wb_ref, cs_ref, sn_ref, gkv_ref, wukv_ref,
                   qcat_ref, lat_ref, kr_ref, kcat_ref, v_ref):
    xn = _rms(x_ref[...], g_ref[...]).astype(BF16)
    pb = jnp.dot(xn, wb_ref[...], preferred_element_type=F32)
    cs = cs_ref[...]
    sn = sn_ref[...]
    lane = lax.broadcasted_iota(jnp.int32, cs.shape, 1)
    low = lane < QK_ROPE
    nq = MLA_HEADS * QK_NOPE
    for p in range(MLA_HEADS // 2):
        slab = pb[:, nq + p * LANE:nq + (p + 1) * LANE]
        roped = (slab * cs + _rope_swap(slab) * sn) * Q_SCALE
        for j in range(2):
            h = 2 * p + j
            qn = pb[:, h * QK_NOPE:(h + 1) * QK_NOPE] * Q_SCALE
            qcat_ref[:, h * QK_CAT:h * QK_CAT + LANE] = qn.astype(BF16)
            half = roped if j == 0 else pltpu.roll(roped, QK_ROPE, 1)
            qcat_ref[:, h * QK_CAT + LANE:(h + 1) * QK_CAT] = jnp.where(low, half, 0.0).astype(BF16)
    lat = _rms(pb[:, nq + QR_W:nq + QR_W + KV_LORA], gkv_ref[...])
    lat_ref[...] = lat
    kslab = pb[:, nq + QR_W + KV_LORA:PB_W]
    kroped = jnp.where(low, kslab * cs + _rope_swap(kslab) * sn, 0.0)
    kr_ref[...] = kroped[:, 0:QK_ROPE]
    _expand_store(lat.astype(BF16), kroped.astype(BF16), wukv_ref, kcat_ref, v_ref)


def mla_prep(x, g_mix, w_b, cs, sn, g_kv, w_ukv, tm):
    n = x.shape[0]
    ntab = cs.shape[0] // tm
    full = lambda a: pl.BlockSpec(a.shape, lambda i: (0, 0))
    return pl.pallas_call(
        _mla_prep_body,
        grid=(n // tm,),
        in_specs=[pl.BlockSpec((tm, D_MODEL), lambda i: (i, 0)), full(g_mix), full(w_b),
                  pl.BlockSpec((tm, LANE), lambda i: (i % ntab, 0)),
                  pl.BlockSpec((tm, LANE), lambda i: (i % ntab, 0)), full(g_kv), full(w_ukv)],
        out_specs=[pl.BlockSpec((tm, MLA_HEADS * QK_CAT), lambda i: (i, 0)),
                   pl.BlockSpec((tm, KV_LORA), lambda i: (i, 0)),
                   pl.BlockSpec((tm, QK_ROPE), lambda i: (i, 0)),
                   pl.BlockSpec((tm, MLA_HEADS * QK_CAT), lambda i: (i, 0)),
                   pl.BlockSpec((tm, MLA_HEADS * V_HEAD), lambda i: (i, 0))],
        out_shape=[jax.ShapeDtypeStruct((n, MLA_HEADS * QK_CAT), BF16),
                   jax.ShapeDtypeStruct((n, KV_LORA), F32),
                   jax.ShapeDtypeStruct((n, QK_ROPE), F32),
                   jax.ShapeDtypeStruct((n, MLA_HEADS * QK_CAT), BF16),
                   jax.ShapeDtypeStruct((n, MLA_HEADS * V_HEAD), BF16)],
        compiler_params=_cparams(("parallel",)),
        name="mla_prep",
    )(x, g_mix, w_b, cs, sn, g_kv, w_ukv)


def _softmax_block(s, v, m, l, acc):
    m_new = jnp.maximum(m, jnp.max(s, axis=-1, keepdims=True))
    alpha = jnp.exp2(m - m_new)
    p = jnp.exp2(s - m_new)
    l_new = alpha * l + jnp.sum(p, axis=-1, keepdims=True)
    acc_new = alpha * acc + jnp.dot(p.astype(BF16), v, preferred_element_type=F32)
    return m_new, l_new, acc_new


def _scores(q, kblk):
    return lax.dot_general(q, kblk, (((1,), (1,)), ((), ())), preferred_element_type=F32)


def _qk_cols(h):
    return slice(h * QK_CAT, (h + 1) * QK_CAT)


def _v_cols(h):
    return slice(h * V_HEAD, (h + 1) * V_HEAD)


def _attn_prompt_body(q_ref, k_ref, v_ref, o_ref, *, tq, hb, nq):
    i = pl.program_id(2)
    qc = lax.broadcasted_iota(jnp.int32, (tq, tq), 0) // CHUNK
    kc = lax.broadcasted_iota(jnp.int32, (tq, tq), 1) // CHUNK
    visible = kc <= qc

    def run(ii):
        start = ii * tq
        carry = []
        for h in range(hb):
            s = jnp.where(visible, _scores(q_ref[:, _qk_cols(h)], k_ref[start:start + tq, _qk_cols(h)]), NEG_INF)
            m = jnp.max(s, axis=-1, keepdims=True)
            p = jnp.exp2(s - m)
            carry.append((m, jnp.sum(p, axis=-1, keepdims=True),
                          jnp.dot(p.astype(BF16), v_ref[start:start + tq, _v_cols(h)], preferred_element_type=F32)))
        for j in range(ii):
            off = j * tq
            carry = [_softmax_block(_scores(q_ref[:, _qk_cols(h)], k_ref[off:off + tq, _qk_cols(h)]),
                                    v_ref[off:off + tq, _v_cols(h)], *carry[h]) for h in range(hb)]
        for h in range(hb):
            m, l, acc = carry[h]
            o_ref[:, _v_cols(h)] = (acc / l).astype(o_ref.dtype)

    for ii in range(nq):
        pl.when(i == ii)(functools.partial(run, ii))


def attention_prompt(qcat, kcat, v, bsz, t, tq, hb):
    nq = t // tq
    return pl.pallas_call(
        functools.partial(_attn_prompt_body, tq=tq, hb=hb, nq=nq),
        grid=(bsz, MLA_HEADS // hb, nq),
        in_specs=[pl.BlockSpec((tq, hb * QK_CAT), lambda b, h, i: (b * nq + i, h)),
                  pl.BlockSpec((t, hb * QK_CAT), lambda b, h, i: (b, h)),
                  pl.BlockSpec((t, hb * V_HEAD), lambda b, h, i: (b, h))],
        out_specs=pl.BlockSpec((tq, hb * V_HEAD), lambda b, h, i: (b * nq + i, h)),
        out_shape=jax.ShapeDtypeStruct((bsz * t, MLA_HEADS * V_HEAD), BF16),
        compiler_params=_cparams(("parallel", "parallel", "arbitrary")),
        name="attention_prompt",
    )(qcat, kcat, v)


def _attn_sample_body(q_ref, latp_ref, krp_ref, latn_ref, krn_ref, pad_ref, wukv_ref, o_ref):
    t = q_ref.shape[0]
    nq = MLA_HEADS * QK_NOPE
    q_lat = jnp.concatenate(
        [_scores(q_ref[:, h * QK_CAT:h * QK_CAT + LANE], wukv_ref[:, h * QK_NOPE:(h + 1) * QK_NOPE])
         for h in range(MLA_HEADS)], axis=0).astype(BF16)
    q_rope = jnp.concatenate([q_ref[:, h * QK_CAT + LANE:(h + 1) * QK_CAT] for h in range(MLA_HEADS)], axis=0)
    lat_p = latp_ref[...].astype(BF16)
    lat_n = latn_ref[...].astype(BF16)
    ks_p = jnp.dot(krp_ref[...].astype(BF16), pad_ref[...], preferred_element_type=F32).astype(BF16)
    ks_n = jnp.dot(krn_ref[...].astype(BF16), pad_ref[...], preferred_element_type=F32).astype(BF16)
    s_p = _scores(q_lat, lat_p) + _scores(q_rope, ks_p)
    s_n = _scores(q_lat, lat_n) + _scores(q_rope, ks_n)
    m = jnp.maximum(jnp.max(s_p, axis=-1, keepdims=True), jnp.max(s_n, axis=-1, keepdims=True))
    p_p = jnp.exp2(s_p - m)
    p_n = jnp.exp2(s_n - m)
    l = jnp.sum(p_p, axis=-1, keepdims=True) + jnp.sum(p_n, axis=-1, keepdims=True)
    ctx = (jnp.dot(p_p.astype(BF16), lat_p, preferred_element_type=F32)
           + jnp.dot(p_n.astype(BF16), lat_n, preferred_element_type=F32)) / l
    ctx = ctx.astype(BF16)
    for h in range(MLA_HEADS):
        o_ref[:, _v_cols(h)] = jnp.dot(ctx[h * t:(h + 1) * t, :], wukv_ref[:, nq + h * V_HEAD:nq + (h + 1) * V_HEAD],
                                       preferred_element_type=F32).astype(o_ref.dtype)


def attention_sample(qcat, lat_past, kr_past, lat_new, kr_new, w_ukv, bsz, t, past):
    v_w = MLA_HEADS * V_HEAD
    pad = jnp.eye(QK_ROPE, LANE, dtype=BF16)
    return pl.pallas_call(
        _attn_sample_body,
        grid=(bsz,),
        in_specs=[pl.BlockSpec((t, MLA_HEADS * QK_CAT), lambda b: (b, 0)),
                  pl.BlockSpec((past, KV_LORA), lambda b: (b, 0)),
                  pl.BlockSpec((past, QK_ROPE), lambda b: (b, 0)),
                  pl.BlockSpec((t, KV_LORA), lambda b: (b, 0)),
                  pl.BlockSpec((t, QK_ROPE), lambda b: (b, 0)),
                  _resident(pad), _resident(w_ukv)],
        out_specs=pl.BlockSpec((t, v_w), lambda b: (b, 0)),
        out_shape=jax.ShapeDtypeStruct((bsz * t, v_w), BF16),
        compiler_params=_cparams(("parallel",)),
        name="attention_sample",
    )(qcat, lat_past, kr_past, lat_new, kr_new, pad, w_ukv)


def _combine_body(ya_ref, ob_ref, pg_ref, x_ref, woa_ref, wob_ref, wout_ref, gffn_ref, h_ref, hn_ref):
    y_a = jnp.dot(ya_ref[...], woa_ref[...], preferred_element_type=F32)
    y_b = jnp.dot(ob_ref[...], wob_ref[...], preferred_element_type=F32)
    gate_a = jax.nn.sigmoid(pg_ref[:, 0:D_MODEL])
    gate_b = jax.nn.sigmoid(pg_ref[:, D_MODEL:2 * D_MODEL])
    mix = (gate_a * y_a + gate_b * y_b).astype(BF16)
    h = x_ref[...] + jnp.dot(mix, wout_ref[...], preferred_element_type=F32)
    h_ref[...] = h
    hn_ref[...] = _rms(h, gffn_ref[...]).astype(BF16)


def combine(ya, ob, pg, x, w_o_a, w_o_b, w_out, g_ffn, tm):
    n = x.shape[0]
    row = lambda w: pl.BlockSpec((tm, w), lambda i: (i, 0))
    res = lambda a: pl.BlockSpec(a.shape, lambda i: (0, 0), pipeline_mode=pl.Buffered(1))
    return pl.pallas_call(
        _combine_body,
        grid=(n // tm,),
        in_specs=[row(RW), row(MLA_HEADS * V_HEAD), row(2 * D_MODEL), row(D_MODEL),
                  res(w_o_a), res(w_o_b), res(w_out), res(g_ffn)],
        out_specs=[row(D_MODEL), row(D_MODEL)],
        out_shape=[jax.ShapeDtypeStruct((n, D_MODEL), F32), jax.ShapeDtypeStruct((n, D_MODEL), BF16)],
        compiler_params=_cparams(("parallel",)),
        name="combine",
    )(ya, ob, pg, x, w_o_a, w_o_b, w_out, g_ffn)


HALO = 16


def _ffn_in_body(hn_ref, halo_ref, wg_ref, wu_ref, cwb_ref, st_ref, act_ref, cnew_ref, wgb_ref, wub_ref, *,
                 tm, ts, tiles_per_seq):
    i = pl.program_id(1)
    nseq = tm // ts
    tn = act_ref.shape[1]

    @pl.when(i == 0)
    def _():
        wgb_ref[...] = wg_ref[...].astype(BF16)
        wub_ref[...] = wu_ref[...].astype(BF16)

    hn = hn_ref[...]
    t_in = lax.broadcasted_iota(jnp.int32, (tm, 1), 0) % ts
    gate = jnp.dot(hn, wgb_ref[...], preferred_element_type=F32)
    up = jnp.dot(hn, wub_ref[...], preferred_element_type=F32)
    if nseq == 1:
        prev2, prev1 = st_ref[0, 0:1, :], st_ref[0, 1:2, :]
    else:
        prev2 = jnp.broadcast_to(st_ref[:, 0:1, :], (nseq, ts, tn)).reshape(tm, tn)
        prev1 = jnp.broadcast_to(st_ref[:, 1:2, :], (nseq, ts, tn)).reshape(tm, tn)
    if tiles_per_seq > 1:
        first = (i % tiles_per_seq) == 0
        ghalo = jnp.dot(halo_ref[...], wgb_ref[...], preferred_element_type=F32)
        prev2 = jnp.where(first, prev2, ghalo[HALO - 2:HALO - 1, :])
        prev1 = jnp.where(first, prev1, ghalo[HALO - 1:HALO, :])
    g1 = jnp.where(t_in == 0, prev1, pltpu.roll(gate, 1, 0))
    g2 = jnp.where(t_in == 0, prev2, jnp.where(t_in == 1, prev1, pltpu.roll(gate, 2, 0)))
    gate_c = g2 * cwb_ref[0:1, :] + g1 * cwb_ref[1:2, :] + gate * cwb_ref[2:3, :] + cwb_ref[3:4, :]
    act_ref[...] = (jax.nn.silu(gate_c) * up).astype(act_ref.dtype)

    @pl.when((i % tiles_per_seq) == tiles_per_seq - 1)
    def _():
        cnew_ref[...] = gate.reshape(nseq, ts, tn)[:, ts - (CONV_W - 1):ts, :]


def ffn_in(hn, w_ffn_in, cwb, conv_state, bsz, t, tm, tn):
    n = bsz * t
    ts = min(t, tm)
    tps = t // ts
    nseq = tm // ts
    ncol = D_FF // tn
    r = tm // HALO
    return pl.pallas_call(
        functools.partial(_ffn_in_body, tm=tm, ts=ts, tiles_per_seq=tps),
        grid=(ncol, n // tm),
        in_specs=[pl.BlockSpec((tm, D_MODEL), lambda j, i: (i, 0)),
                  pl.BlockSpec((HALO, D_MODEL), lambda j, i: (jnp.maximum(i * r - 1, 0), 0)),
                  pl.BlockSpec((D_MODEL, tn), lambda j, i: (0, j)),
                  pl.BlockSpec((D_MODEL, tn), lambda j, i: (0, j + ncol)),
                  pl.BlockSpec((8, tn), lambda j, i: (0, j)),
                  pl.BlockSpec((nseq, 8, tn), lambda j, i: (i // tps, 0, j))],
        out_specs=[pl.BlockSpec((tm, tn), lambda j, i: (i, j)),
                   pl.BlockSpec((nseq, CONV_W - 1, tn), lambda j, i: (i // tps, 0, j))],
        out_shape=[jax.ShapeDtypeStruct((n, D_FF), BF16),
                   jax.ShapeDtypeStruct((bsz, CONV_W - 1, D_FF), F32)],
        scratch_shapes=[pltpu.VMEM((D_MODEL, tn), BF16), pltpu.VMEM((D_MODEL, tn), BF16)],
        compiler_params=_cparams(("parallel", "arbitrary")),
        name="ffn_in",
    )(hn, hn, w_ffn_in, w_ffn_in, cwb, conv_state)


def _ffn_down_body(act_ref, w_ref, h_ref, g_ref, y_ref):
    y_ref[...] = _rms(h_ref[...] + jnp.dot(act_ref[...], w_ref[...], preferred_element_type=F32), g_ref[...])


def ffn_down(act, w_down, h, g_final, tm):
    n = h.shape[0]
    return pl.pallas_call(
        _ffn_down_body,
        grid=(n // tm,),
        in_specs=[pl.BlockSpec((tm, D_FF), lambda i: (i, 0)), _resident(w_down),
                  pl.BlockSpec((tm, D_MODEL), lambda i: (i, 0)), _resident(g_final)],
        out_specs=pl.BlockSpec((tm, D_MODEL), lambda i: (i, 0)),
        out_shape=jax.ShapeDtypeStruct((n, D_MODEL), F32),
        compiler_params=_cparams(("parallel",)),
        name="ffn_down",
    )(act, w_down, h, g_final)


def _pad_cols(w, width):
    return jnp.pad(w, ((0, 0), (0, width - w.shape[1])))


def _prepare_params(g_mix, w_in, mu_shift, w0, w2, a0, a2, g2, k_k, k_a, r_k, lnx_g, lnx_b, w_o_a, g_kv,
                    w_ukv, w_o_b, w_out, g_ffn, w_ffn_in, conv_w, conv_b, w_ffn_down, g_final):
    o_zw = 3 * RW
    o_za = o_zw + DECAY_LORA
    o_zg = o_za + AAA_LORA
    o_q = RWKV_PROJ
    o_ckv = o_q + MLA_Q_WIDTH
    o_kr = o_ckv + KV_LORA
    o_g = RWKV_PROJ + MLA_PROJ

    def sect_a(m):
        return jnp.concatenate([m[:, :o_zw], _pad_cols(m[:, o_zw:o_za], LORA_PAD),
                                _pad_cols(m[:, o_za:o_zg], LORA_PAD), m[:, o_zg:RWKV_PROJ]], axis=1)

    wq = w_in[:, o_q:o_ckv].reshape(D_MODEL, MLA_HEADS, QK_NOPE + QK_ROPE)
    w_b = jnp.concatenate([wq[:, :, :QK_NOPE].reshape(D_MODEL, -1), wq[:, :, QK_NOPE:].reshape(D_MODEL, -1),
                           w_in[:, o_ckv:o_kr], _pad_cols(w_in[:, o_kr:o_g], LANE)], axis=1)
    wkv = w_ukv.reshape(KV_LORA, MLA_HEADS, QK_NOPE + V_HEAD)
    w_ukv_p = jnp.concatenate([wkv[:, :, :QK_NOPE].reshape(KV_LORA, -1),
                               wkv[:, :, QK_NOPE:].reshape(KV_LORA, -1)], axis=1)
    pad_rows = lambda m: jnp.pad(m, ((0, LORA_PAD - m.shape[0]), (0, 0)))
    row = lambda vct: vct.reshape(1, -1)
    cwb = jnp.concatenate([conv_w, conv_b.reshape(1, D_FF), jnp.zeros((8 - CONV_W - 1, D_FF), F32)], axis=0)
    return dict(
        g_mix=row(g_mix), w_a=sect_a(w_in[:, :RWKV_PROJ]).astype(BF16), w_b=w_b.astype(BF16),
        w_g=w_in[:, o_g:].astype(BF16), mu_a=sect_a(row(mu_shift)),
        w0=row(w0), w2=pad_rows(w2).astype(BF16), a0=row(a0), a2=pad_rows(a2).astype(BF16),
        g2=g2.astype(BF16), k_k=row(k_k), k_a=row(k_a), r_k=row(r_k.reshape(-1)),
        lnx_g=row(lnx_g), lnx_b=row(lnx_b), w_o_a=w_o_a.astype(BF16), g_kv=row(g_kv),
        w_ukv=w_ukv_p.astype(BF16), w_o_b=w_o_b.astype(BF16), w_out=w_out.astype(BF16),
        g_ffn=row(g_ffn), w_ffn_in=w_ffn_in, cwb=cwb, w_ffn_down=w_ffn_down.astype(BF16),
        g_final=row(g_final), sect_a=sect_a)


def _rope_tables(pos):
    half = QK_ROPE // 2
    inv = ROPE_THETA ** (-jnp.arange(half, dtype=F32) / half)
    ang = pos.astype(F32)[:, None] * inv[None, :]
    cos, sin = jnp.cos(ang), jnp.sin(ang)
    return jnp.tile(cos, (1, 4)), jnp.tile(jnp.concatenate([-sin, sin], axis=1), (1, 2))


def _tile_rows(n, pref):
    return pref if n % pref == 0 else n


def _layer(x, pos, shift_prev, wkv0, conv_prev, past, prm):
    bsz, t, _ = x.shape
    n = bsz * t
    x2 = x.reshape(n, D_MODEL)
    tm = _tile_rows(n, 256)

    pa = norm_matmul(x2, prm["g_mix"], prm["w_a"], _tile_rows(n, 512))

    shift_p = jnp.pad(prm["sect_a"](shift_prev.reshape(bsz, RWKV_PROJ)).reshape(bsz, 1, PA_W),
                      ((0, 0), (0, 7), (0, 0)))
    if t >= 256:
        ya, wkv_new, pg = rwkv_mix(pa, shift_p, wkv0, x2, prm, bsz, t, 64, 4)
    else:
        group = 4 if (t <= 64 and bsz % 4 == 0) else (2 if (t <= 64 and bsz % 2 == 0) else 1)
        ya, wkv_new = rwkv_mix(pa, shift_p, wkv0, None, prm, bsz, t, min(t, 64), group, separate=t <= 64)
        pg = norm_matmul(x2, prm["g_mix"], prm["w_g"], tm)
    last = pa.reshape(bsz, t, PA_W)[:, t - 1]
    o_zw = 3 * RW
    shift_new = jnp.concatenate([last[:, :o_zw], last[:, o_zw:o_zw + DECAY_LORA],
                                 last[:, o_zw + LORA_PAD:o_zw + LORA_PAD + AAA_LORA],
                                 last[:, o_zw + 2 * LORA_PAD:]], axis=1).reshape(bsz, 1, RWKV_PROJ)

    cs, sn = _rope_tables(pos)
    if past is None:
        tmb = _tile_rows(t, 512)
    else:
        tmb = n
        cs, sn = jnp.tile(cs, (bsz, 1)), jnp.tile(sn, (bsz, 1))
    qcat, lat, kr, kcat, v = mla_prep(x2, prm["g_mix"], prm["w_b"], cs, sn, prm["g_kv"], prm["w_ukv"], tmb)
    if past is None:
        ob = attention_prompt(qcat, kcat, v, bsz, t, 512, 4)
    else:
        plat, pkr = past
        plen = plat.shape[1]
        ob = attention_sample(qcat, plat.reshape(bsz * plen, KV_LORA), pkr.reshape(bsz * plen, QK_ROPE),
                              lat, kr, prm["w_ukv"], bsz, t, plen)

    h, hn = combine(ya, ob, pg, x2, prm["w_o_a"], prm["w_o_b"], prm["w_out"], prm["g_ffn"], tm)
    conv_p = jnp.pad(conv_prev, ((0, 0), (0, 8 - (CONV_W - 1)), (0, 0)))
    tm_ffn = _tile_rows(n, 1024) if t >= 1024 else _tile_rows(n, 512)
    act, conv_new = ffn_in(hn, prm["w_ffn_in"], prm["cwb"], conv_p, bsz, t, tm_ffn, 512)
    y = ffn_down(act, prm["w_ffn_down"], h, prm["g_final"], _tile_rows(n, 512))
    return (y.reshape(bsz, t, D_MODEL), lat.reshape(bsz, t, KV_LORA), kr.reshape(bsz, t, QK_ROPE),
            wkv_new, shift_new, conv_new)


def kernel(x_prompt, x_sample, cache_mla_latent, cache_mla_krope, state_rwkv_wkv, state_rwkv_shift, state_ffn_conv, g_mix, w_in, mu_shift, w0, w2, a0, a2, g2, k_k, k_a, r_k, lnx_g, lnx_b, w_o_a, g_kv, w_ukv, w_o_b, w_out, g_ffn, w_ffn_in, conv_w, conv_b, w_ffn_down, g_final):
    depth = w_in.shape[0]
    assert depth == 1, "single-layer step"
    bp, tp, _ = x_prompt.shape
    ts = x_sample.shape[1]
    past = cache_mla_latent.shape[2]
    prm = _prepare_params(g_mix[0], w_in[0], mu_shift[0], w0[0], w2[0], a0[0], a2[0], g2[0], k_k[0], k_a[0],
                          r_k[0], lnx_g[0], lnx_b[0], w_o_a[0], g_kv[0], w_ukv[0], w_o_b[0], w_out[0],
                          g_ffn[0], w_ffn_in[0], conv_w[0], conv_b[0], w_ffn_down[0], g_final)
    dt = x_prompt.dtype
    out_p = _layer(x_prompt, jnp.arange(tp), jnp.zeros((bp, 1, RWKV_PROJ), dt),
                   jnp.zeros((bp, N_HEADS, HEAD, HEAD), F32), jnp.zeros((bp, CONV_W - 1, D_FF), dt),
                   None, prm)
    out_s = _layer(x_sample, past + jnp.arange(ts), state_rwkv_shift[0], state_rwkv_wkv[0],
                   state_ffn_conv[0], (cache_mla_latent[0], cache_mla_krope[0]), prm)
    lead = lambda a: a[None].astype(dt)
    return (out_p[0], out_s[0],
            lead(out_p[1]), lead(out_p[2]), lead(out_p[3]), lead(out_p[4]), lead(out_p[5]),
            lead(out_s[1]), lead(out_s[2]), lead(out_s[3]), lead(out_s[4]), lead(out_s[5]))
```
